```python
import functools
import jax, jax.numpy as jnp
from jax import lax
import numpy as np

D_MODEL = 2048
BATCH = 16
SEQ = 256
DEPTH = 4
DEC_BATCH = 2
DEC_SEQ = 1024
PAST_LEN = 512

N_MIXERS = 3
N_A = (DEPTH + 2) // 3
N_B = (DEPTH + 1) // 3
N_C = DEPTH // 3
N_MOD = 9
GRID_W = 64
EPS = 1e-6
D_FF = 5632
A_CHUNK = 128
A_WIDTH = 3 * D_MODEL
A_GROUPS = 8
R_HEADS = 8
R_DK = D_MODEL // R_HEADS
R_DV = 2 * R_DK
R_PROJ = 2 * R_HEADS * R_DK + 2 * R_HEADS * R_DV
R_CHUNK = 128
ROPE_BASE = 10000.0
LRU_BLOCK = 256
LRU_WIDTH = (4 * D_MODEL // 3) // LRU_BLOCK * LRU_BLOCK
LRU_BLOCKS = LRU_WIDTH // LRU_BLOCK
CONV_W = 4
LRU_C = 8.0

kernel_name = 'hybrid_flow_trunk_step'


def rms_norm(x, g):
    x32 = x.astype(jnp.float32)
    y = x32 * lax.rsqrt(jnp.mean(x32 * x32, axis=-1, keepdims=True) + EPS)
    return (y * g.astype(jnp.float32)).astype(x.dtype)


def modulate(h, shift, scale):
    return h * (1 + scale) + shift


def swiglu(h, w_in, w_out):
    gate, up = jnp.split(h @ w_in, 2, axis=-1)
    return (jax.nn.silu(gate) * up) @ w_out


def chunk_mlp_mix(h, w_in, v_g, w_s, b_s, w_out):
    bsz, L, _ = h.shape
    u, v = jnp.split(jax.nn.gelu(h @ w_in), 2, axis=-1)
    v = rms_norm(v, v_g).reshape(bsz, L // A_CHUNK, A_CHUNK, A_GROUPS, A_WIDTH // A_GROUPS)
    sv = jnp.einsum('gpq,bnqgc->bnpgc', w_s, v) + b_s.T[:, :, None]
    return (u * sv.reshape(bsz, L, A_WIDTH)) @ w_out, None


def rope_2d(x, row, col):
    nf = R_DK // 4
    inv = ROPE_BASE ** (-jnp.arange(nf, dtype=jnp.float32) / nf)

    def rot(xp, p):
        ang = p.astype(jnp.float32)[:, None] * inv
        cos = jnp.cos(ang)[None, :, None, :]
        sin = jnp.sin(ang)[None, :, None, :]
        x1, x2 = xp[..., :nf], xp[..., nf:]
        return jnp.concatenate([x1 * cos - x2 * sin, x1 * sin + x2 * cos], axis=-1)

    half = R_DK // 2
    return jnp.concatenate([rot(x[..., :half], row), rot(x[..., half:], col)], axis=-1)


def retention_scan(q, k, v, log_g, s0):
    bsz, L, H, _ = q.shape
    n = L // R_CHUNK

    def to_chunks(t):
        return t.reshape(bsz, n, R_CHUNK, H, t.shape[-1]).transpose(1, 0, 3, 2, 4)

    qc, kc, vc = to_chunks(q), to_chunks(k), to_chunks(v)
    pos = jnp.arange(R_CHUNK, dtype=jnp.float32)
    rel = pos[:, None] - pos[None, :]
    dmat = jnp.where(rel >= 0, jnp.exp(jnp.maximum(rel, 0.0)[None] * log_g[:, None, None]), 0.0)
    scores = jnp.einsum('nbhcd,nbhed->nbhce', qc, kc) * dmat
    o_inner = jnp.einsum('nbhce,nbhev->nbhcv', scores, vc)
    zeta = jnp.exp((R_CHUNK - 1 - pos)[None] * log_g[:, None])
    xi = jnp.exp((pos + 1)[None] * log_g[:, None])
    g_chunk = jnp.exp(R_CHUNK * log_g)[:, None, None]
    kv = jnp.einsum('nbhcd,nbhcv->nbhdv', kc * zeta[:, :, None], vc)

    def step(R, inp):
        q_i, kv_i = inp
        o = jnp.einsum('bhcd,bhdv->bhcv', q_i, R) * xi[None, :, :, None]
        return g_chunk * R + kv_i, o

    R_fin, o_cross = lax.scan(step, s0, (qc, kv))
    o = (o_inner + o_cross).transpose(1, 0, 3, 2, 4).reshape(bsz, L, H, v.shape[-1])
    return o, R_fin


def retention_mix(h, w_in, decay_logit, gn_g, w_out, s0, pos):
    bsz, L, _ = h.shape
    qk_w, v_w = R_HEADS * R_DK, R_HEADS * R_DV
    q, k, v, g = jnp.split(h @ w_in, [qk_w, 2 * qk_w, 2 * qk_w + v_w], axis=-1)
    q = q.astype(jnp.float32).reshape(bsz, L, R_HEADS, R_DK)
    k = k.astype(jnp.float32).reshape(bsz, L, R_HEADS, R_DK)
    v = v.astype(jnp.float32).reshape(bsz, L, R_HEADS, R_DV)
    if pos is not None:
        q = rope_2d(q, pos[0], pos[1])
        k = rope_2d(k, pos[0], pos[1])
    k = k * R_DK ** -0.5
    log_g = jax.nn.log_sigmoid(decay_logit.astype(jnp.float32))
    s0 = s0.astype(jnp.float32)
    o_f, s_f = retention_scan(q, k, v, log_g[0], s0[:, 0])
    o_b, s_b = retention_scan(jnp.flip(q, 1), jnp.flip(k, 1), jnp.flip(v, 1), log_g[1], s0[:, 1])
    o = o_f + jnp.flip(o_b, 1)
    mu = jnp.mean(o, axis=-1, keepdims=True)
    var = jnp.mean(jnp.square(o - mu), axis=-1, keepdims=True)
    o = ((o - mu) * lax.rsqrt(var + EPS)).reshape(bsz, L, v_w) * gn_g.astype(jnp.float32)
    y = (jax.nn.silu(g) * o.astype(h.dtype)) @ w_out
    return y, jnp.stack([s_f, s_b], axis=1)


def linear_scan(a, b, h0):
    a_cum, b_cum = lax.associative_scan(
        lambda l, r: (l[0] * r[0], r[0] * l[1] + r[1]), (a, b), axis=1)
    return b_cum + a_cum * h0[:, None, :]


def rglru_mix(h, w_in, conv_w, conv_b, gate_w, gate_b, lam, w_out, h0):
    bsz, L, _ = h.shape
    y_br, x_br = jnp.split(h @ w_in, 2, axis=-1)
    pad_l = CONV_W // 2
    xp = jnp.pad(x_br, ((0, 0), (pad_l, CONV_W - 1 - pad_l), (0, 0)))
    xc = sum(xp[:, j:j + L] * conv_w[j] for j in range(CONV_W)) + conv_b
    xb = xc.reshape(bsz, L, LRU_BLOCKS, LRU_BLOCK)
    gates = jnp.einsum('blnj,dgnjk->dgblnk', xb, gate_w).reshape(2, 2, bsz, L, LRU_WIDTH)
    gates = jax.nn.sigmoid((gates + gate_b[:, :, None, None, :]).astype(jnp.float32))
    r, i = gates[:, 0], gates[:, 1]
    log_a = LRU_C * r * jax.nn.log_sigmoid(lam.astype(jnp.float32))[:, None, None, :]
    a = jnp.exp(log_a)
    b = jnp.sqrt(-jnp.expm1(2.0 * log_a)) * i * xc.astype(jnp.float32)[None]
    h0 = h0.astype(jnp.float32)
    h_f = linear_scan(a[0], b[0], h0[:, 0])
    h_b = jnp.flip(linear_scan(jnp.flip(a[1], 1), jnp.flip(b[1], 1), h0[:, 1]), 1)
    states = jnp.stack([h_f[:, -1], h_b[:, 0]], axis=1)
    y = ((h_f + h_b).astype(h.dtype) * jax.nn.gelu(y_br)) @ w_out
    return y, states


def layer(x, mod, g, ffn_in, ffn_out, mixer):
    m = lambda idx: mod[:, None, idx, :]
    h = modulate(rms_norm(x, g[0]), m(0), m(1))
    x = x + 0.5 * m(2) * swiglu(h, ffn_in[0], ffn_out[0])
    h = modulate(rms_norm(x, g[1]), m(3), m(4))
    y, st = mixer(h)
    x = x + m(5) * y
    h = modulate(rms_norm(x, g[2]), m(6), m(7))
    x = x + 0.5 * m(8) * swiglu(h, ffn_in[1], ffn_out[1])
    return x, st


def setup_inputs(seed: int = 0) -> dict:
    key = jax.random.key(seed)
    keys = jax.random.split(key, 32)
    f32 = jnp.float32

    def nrm(idx, shape, scale=1.0):
        return scale * jax.random.normal(keys[idx], shape, f32)

    D = D_MODEL
    h_idx = jnp.arange(R_HEADS, dtype=f32)
    gamma_logit = jnp.log1p(-jnp.exp2(-5.0 - h_idx)) + (5.0 + h_idx) * jnp.log(2.0)
    u = jax.random.uniform(keys[26], (N_C, 2, LRU_WIDTH), f32, 0.9, 0.999)
    s = u ** (1.0 / LRU_C)
    lam = jnp.log(s) - jnp.log1p(-s)
    return {
        'x_prompt': nrm(0, (BATCH, SEQ, D)),
        'x_sample': nrm(1, (DEC_BATCH, DEC_SEQ, D)),
        'state_ret': nrm(2, (DEC_BATCH, N_B, 2, R_HEADS, R_DK, R_DV), 0.3),
        'state_lru': nrm(3, (DEC_BATCH, N_C, 2, LRU_WIDTH), 0.5),
        'c': nrm(4, (DEC_BATCH, D)),
        'c_ctx': nrm(5, (D,)),
        'norm_g': 1.0 + nrm(6, (DEPTH, 3, D), 0.02),
        'final_g': 1.0 + nrm(7, (D,), 0.02),
        'w_mod': nrm(8, (DEPTH, D, N_MOD * D), 0.5 * D ** -0.5),
        'b_mod': nrm(9, (DEPTH, N_MOD * D), 0.02),
        'ffn_w_in': nrm(10, (DEPTH, 2, D, 2 * D_FF), D ** -0.5),
        'ffn_w_out': nrm(11, (DEPTH, 2, D_FF, D), D_FF ** -0.5),
        'a_w_in': nrm(12, (N_A, D, 2 * A_WIDTH), D ** -0.5),
        'a_v_g': 1.0 + nrm(13, (N_A, A_WIDTH), 0.02),
        'a_w_s': nrm(14, (N_A, A_GROUPS, A_CHUNK, A_CHUNK), A_CHUNK ** -0.5),
        'a_b_s': 1.0 + nrm(15, (N_A, A_GROUPS, A_CHUNK), 0.02),
        'a_w_out': nrm(16, (N_A, A_WIDTH, D), A_WIDTH ** -0.5),
        'r_w_in': nrm(17, (N_B, D, R_PROJ), D ** -0.5),
        'r_decay': gamma_logit + nrm(18, (N_B, 2, R_HEADS), 0.1),
        'r_gn_g': 1.0 + nrm(19, (N_B, R_HEADS * R_DV), 0.02),
        'r_w_out': nrm(20, (N_B, R_HEADS * R_DV, D), (R_HEADS * R_DV) ** -0.5),
        'l_w_in': nrm(21, (N_C, D, 2 * LRU_WIDTH), D ** -0.5),
        'l_conv_w': nrm(22, (N_C, CONV_W, LRU_WIDTH), CONV_W ** -0.5),
        'l_conv_b': nrm(23, (N_C, LRU_WIDTH), 0.02),
        'l_gate_w': nrm(24, (N_C, 2, 2, LRU_BLOCKS, LRU_BLOCK, LRU_BLOCK), LRU_BLOCK ** -0.5),
        'l_gate_b': nrm(25, (N_C, 2, 2, LRU_WIDTH), 0.02),
        'l_lambda': lam,
        'l_w_out': nrm(27, (N_C, LRU_WIDTH, D), LRU_WIDTH ** -0.5),
    }


def reference(x_prompt, x_sample, state_ret, state_lru, c, c_ctx, norm_g, final_g, w_mod, b_mod,
              ffn_w_in, ffn_w_out, a_w_in, a_v_g, a_w_s, a_b_s, a_w_out, r_w_in, r_decay, r_gn_g,
              r_w_out, l_w_in, l_conv_w, l_conv_b, l_gate_w, l_gate_b, l_lambda, l_w_out):

    def trunk(x, cond, ret_s0, lru_s0, pos):
        bsz = x.shape[0]
        ret_out, lru_out = [], []
        for l in range(DEPTH):
            kind, j = l % N_MIXERS, l // N_MIXERS
            mod = (jax.nn.silu(cond) @ w_mod[l] + b_mod[l]).reshape(cond.shape[0], N_MOD, D_MODEL)
            if kind == 0:
                mixer = functools.partial(chunk_mlp_mix, w_in=a_w_in[j], v_g=a_v_g[j],
                                          w_s=a_w_s[j], b_s=a_b_s[j], w_out=a_w_out[j])
            elif kind == 1:
                s0 = (jnp.zeros((bsz, 2, R_HEADS, R_DK, R_DV), jnp.float32)
                      if ret_s0 is None else ret_s0[:, j])
                mixer = functools.partial(retention_mix, w_in=r_w_in[j], decay_logit=r_decay[j],
                                          gn_g=r_gn_g[j], w_out=r_w_out[j], s0=s0, pos=pos)
            else:
                h0 = (jnp.zeros((bsz, 2, LRU_WIDTH), jnp.float32)
                      if lru_s0 is None else lru_s0[:, j])
                mixer = functools.partial(rglru_mix, w_in=l_w_in[j], conv_w=l_conv_w[j],
                                          conv_b=l_conv_b[j], gate_w=l_gate_w[j],
                                          gate_b=l_gate_b[j], lam=l_lambda[j],
                                          w_out=l_w_out[j], h0=h0)
            x, st = layer(x, mod, norm_g[l], ffn_w_in[l], ffn_w_out[l], mixer)
            if kind == 1:
                ret_out.append(st)
            elif kind == 2:
                lru_out.append(st)
        return rms_norm(x, final_g), ret_out, lru_out

    y_prompt, ret_states, lru_states = trunk(x_prompt, c_ctx[None, :], None, None, None)
    new_state_ret = jnp.stack(ret_states, axis=1).astype(x_prompt.dtype)
    new_state_lru = jnp.stack(lru_states, axis=1).astype(x_prompt.dtype)

    rows = x_sample.shape[1] // GRID_W
    row_ids = jnp.repeat(jnp.arange(rows), GRID_W)
    col_ids = jnp.tile(jnp.arange(GRID_W), rows)
    y_sample, _, _ = trunk(x_sample, c, state_ret, state_lru, (row_ids, col_ids))

    return (y_prompt, y_sample, new_state_ret, new_state_lru)
```

```python
import functools

import jax
import jax.numpy as jnp
from jax import lax
from jax.experimental import pallas as pl
from jax.experimental.pallas import tpu as pltpu

F32 = jnp.float32
BF16 = jnp.bfloat16

EPS = 1e-6
N_MIXERS = 3
N_MOD = 9
GRID_W = 64
A_CHUNK = 128
A_GROUPS = 8
R_HEADS = 8
R_CHUNK = 128
ROPE_BASE = 10000.0
LRU_BLOCK = 256
LRU_C = 8.0

LANES = 128
MOD_ROWS = 8
TM = 1024
ROW_CHUNK = 256
VMEM_LIMIT = 56 * 1024 * 1024


def _params(n_grid):
    return pltpu.CompilerParams(dimension_semantics=("arbitrary",) * n_grid,
                                vmem_limit_bytes=VMEM_LIMIT)


def _lead(lead, block, index_map):
    return pl.BlockSpec((None,) * len(lead) + tuple(block),
                        lambda *g: tuple(lead) + tuple(index_map(*g)))


def _log_sigmoid(x):
    return jnp.minimum(x, 0.0) - jnp.log1p(jnp.exp(-jnp.abs(x)))


def _mod_kernel(c_ref, w_ref, b_ref, o_ref):
    s = jax.nn.silu(c_ref[...]).astype(BF16)
    o_ref[...] = jnp.dot(s, w_ref[...].astype(BF16), preferred_element_type=F32) + b_ref[...]


def _modulation(cond, w_mod, b_mod, tn=1024):
    depth, d, n = w_mod.shape
    return pl.pallas_call(
        _mod_kernel,
        grid=(depth, n // tn),
        in_specs=[pl.BlockSpec((MOD_ROWS, d), lambda l, j: (0, 0)),
                  pl.BlockSpec((None, d, tn), lambda l, j: (l, 0, j)),
                  pl.BlockSpec((None, 1, tn), lambda l, j: (l, 0, j))],
        out_specs=pl.BlockSpec((None, MOD_ROWS, tn), lambda l, j: (l, 0, j)),
        out_shape=jax.ShapeDtypeStruct((depth, MOD_ROWS, n), F32),
        compiler_params=_params(2),
        name="modulation",
    )(cond, w_mod, b_mod.reshape(depth, 1, n))


def _proj_kernel(x_ref, g_ref, mod_ref, *refs, n_w, shift_idx, scale_idx, act):
    w_refs, o_ref, h_ref = refs[:n_w], refs[n_w], refs[n_w + 1]

    @pl.when(pl.program_id(1) == 0)
    def _prologue():
        gain = g_ref[...]
        scale = 1.0 + mod_ref[scale_idx:scale_idx + 1, :]
        shift = mod_ref[shift_idx:shift_idx + 1, :]

        def rows_body(r, carry):
            rows = pl.ds(pl.multiple_of(r * ROW_CHUNK, ROW_CHUNK), ROW_CHUNK)
            x = x_ref[rows, :]
            y = x * lax.rsqrt(jnp.mean(x * x, axis=-1, keepdims=True) + EPS)
            h_ref[rows, :] = ((y * gain) * scale + shift).astype(h_ref.dtype)
            return carry

        lax.fori_loop(0, x_ref.shape[0] // ROW_CHUNK, rows_body, 0)

    h = h_ref[...]
    ys = [jnp.dot(h, w[...].astype(BF16), preferred_element_type=F32) for w in w_refs]
    if act == "swiglu":
        out = jax.nn.silu(ys[0]) * ys[1]
    elif act == "gelu":
        out = jax.nn.gelu(ys[0])
    else:
        out = ys[0]
    o_ref[...] = out.astype(o_ref.dtype)


def _project(x, gain, mod, w, lead, seg, *, shift_idx, scale_idx, act, tn):
    m, d = x.shape
    n_total = w.shape[-1]
    if act == "swiglu":
        n_out = n_total // 2
        half = n_out // tn
        w_specs = [_lead(lead, (d, tn), lambda i, j: (0, j)),
                   _lead(lead, (d, tn), lambda i, j: (0, j + half))]
    else:
        n_out = n_total
        w_specs = [_lead(lead, (d, tn), lambda i, j: (0, j))]
    body = functools.partial(_proj_kernel, n_w=len(w_specs), shift_idx=shift_idx,
                             scale_idx=scale_idx, act=act)
    return pl.pallas_call(
        body,
        grid=(m // TM, n_out // tn),
        in_specs=[pl.BlockSpec((TM, d), lambda i, j: (i, 0)),
                  pl.BlockSpec((1, d), lambda i, j: (0, 0)),
                  pl.BlockSpec((None, N_MOD, d), lambda i, j: (seg(i), 0, 0))] + w_specs,
        out_specs=pl.BlockSpec((TM, tn), lambda i, j: (i, j)),
        out_shape=jax.ShapeDtypeStruct((m, n_out), BF16),
        scratch_shapes=[pltpu.VMEM((TM, d), BF16)],
        compiler_params=_params(2),
        name="project_" + act,
    )(x, gain.reshape(1, d), mod, *([w] * len(w_specs)))


def _out_kernel(a_ref, w_ref, x_ref, mod_ref, o_ref, *, gate_idx, coef):
    y = jnp.dot(a_ref[...], w_ref[...].astype(BF16), preferred_element_type=F32)
    gate = mod_ref[gate_idx:gate_idx + 1, :]
    if coef != 1.0:
        gate = coef * gate
    o_ref[...] = x_ref[...] + gate * y


def _out_project(a, w, lead, x, mod, seg, *, gate_idx, coef, tn=256):
    m, k = a.shape
    d = x.shape[-1]
    body = functools.partial(_out_kernel, gate_idx=gate_idx, coef=coef)
    return pl.pallas_call(
        body,
        grid=(m // TM, d // tn),
        in_specs=[pl.BlockSpec((TM, k), lambda i, j: (i, 0)),
                  _lead(lead, (k, tn), lambda i, j: (0, j)),
                  pl.BlockSpec((TM, tn), lambda i, j: (i, j)),
                  pl.BlockSpec((None, N_MOD, tn), lambda i, j: (seg(i), 0, j))],
        out_specs=pl.BlockSpec((TM, tn), lambda i, j: (i, j)),
        out_shape=jax.ShapeDtypeStruct((m, d), F32),
        compiler_params=_params(2),
        name="out_project",
    )(a, w, x, mod)


def _gmlp_kernel(u_ref, v_ref, vg_ref, ws_ref, bs_ref, o_ref):
    width = v_ref.shape[1]
    gw = width // A_GROUPS
    vg = vg_ref[...]
    for c in range(v_ref.shape[0] // A_CHUNK):
        rows = slice(c * A_CHUNK, (c + 1) * A_CHUNK)
        v = v_ref[rows, :].astype(F32)
        vn = v * lax.rsqrt(jnp.mean(v * v, axis=-1, keepdims=True) + EPS)
        vn = (vn * vg).astype(BF16)
        for g in range(A_GROUPS):
            cols = slice(g * gw, (g + 1) * gw)
            sv = jnp.dot(ws_ref[g].astype(BF16), vn[:, cols], preferred_element_type=F32)
            sv = sv + bs_ref[:, g:g + 1]
            o_ref[rows, cols] = (u_ref[rows, cols].astype(F32) * sv).astype(o_ref.dtype)


def _gmlp_core(uv, v_g, w_s, b_s, j, rows_per_step=2 * A_CHUNK):
    m, two_w = uv.shape
    width = two_w // 2
    tr = rows_per_step
    return pl.pallas_call(
        _gmlp_kernel,
        grid=(m // tr,),
        in_specs=[pl.BlockSpec((tr, width), lambda i: (i, 0)),
                  pl.BlockSpec((tr, width), lambda i: (i, 1)),
                  _lead((j,), (1, width), lambda i: (0, 0)),
                  _lead((j,), (A_GROUPS, A_CHUNK, A_CHUNK), lambda i: (0, 0, 0)),
                  _lead((j,), (A_CHUNK, A_GROUPS), lambda i: (0, 0))],
        out_specs=pl.BlockSpec((tr, width), lambda i: (i, 0)),
        out_shape=jax.ShapeDtypeStruct((m, width), BF16),
        compiler_params=_params(1),
        name="gmlp_core",
    )(uv, uv, v_g[:, None, :], w_s, jnp.swapaxes(b_s, 1, 2))


def _ret_kernel(*refs, seq_len, dk, has_state, has_rope, emit_state):
    it = iter(refs)
    q_ref, k_ref, v_ref, g_ref, dec_ref, gn_ref = (next(it) for _ in range(6))
    cos_ref, sin_ref = (next(it), next(it)) if has_rope else (None, None)
    s0_ref = next(it) if has_state else None
    o_ref = next(it)
    st_ref = next(it) if emit_state else None
    q_scr, k_scr, acc_scr, r_scr = (next(it) for _ in range(4))
    chunk = R_CHUNK
    n_chunks = seq_len // chunk

    q = q_ref[...].astype(F32)
    k = k_ref[...].astype(F32)
    if has_rope:
        cos, sin = cos_ref[...], sin_ref[...]
        half = LANES // 2

        def rope(t):
            swapped = jnp.concatenate(
                [pltpu.roll(t[:, s:s + LANES], half, 1) for s in range(0, dk, LANES)], axis=1)
            return t * cos + swapped * sin

        q, k = rope(q), rope(k)
    q_scr[...] = q.astype(BF16)
    k_scr[...] = k * (dk ** -0.5)

    ri = lax.broadcasted_iota(jnp.int32, (chunk, chunk), 0).astype(F32)
    ci = lax.broadcasted_iota(jnp.int32, (chunk, chunk), 1).astype(F32)
    rel = ri - ci
    gn = gn_ref[...]

    def finalize(rows, o):
        mu = jnp.mean(o, axis=-1, keepdims=True)
        var = jnp.mean(jnp.square(o - mu), axis=-1, keepdims=True)
        on = ((o - mu) * lax.rsqrt(var + EPS)) * gn
        o_ref[rows, :] = (jax.nn.silu(g_ref[rows, :].astype(F32)) * on).astype(o_ref.dtype)

    for direction in range(2):
        lg = _log_sigmoid(dec_ref[direction])
        if direction == 0:
            dmat = jnp.where(rel >= 0, jnp.exp(jnp.maximum(rel, 0.0) * lg), 0.0)
            zeta = jnp.exp((chunk - 1 - ri) * lg)[:, :1]
            xi = jnp.exp((ri + 1) * lg)[:, :1]
            order = range(n_chunks)
        else:
            dmat = jnp.where(rel <= 0, jnp.exp(jnp.maximum(-rel, 0.0) * lg), 0.0)
            zeta = jnp.exp(ri * lg)[:, :1]
            xi = jnp.exp((chunk - ri) * lg)[:, :1]
            order = range(n_chunks - 1, -1, -1)
        g_chunk = jnp.exp(chunk * lg)[:, :1]
        if has_state:
            r_scr[...] = s0_ref[direction].astype(F32)
        else:
            r_scr[...] = jnp.zeros(r_scr.shape, F32)
        for i in order:
            rows = slice(i * chunk, (i + 1) * chunk)
            qi = q_scr[rows, :]
            kf = k_scr[rows, :]
            vi = v_ref[rows, :]
            s = lax.dot_general(qi, kf.astype(BF16), (((1,), (1,)), ((), ())),
                                preferred_element_type=F32)
            o_inner = jnp.dot((s * dmat).astype(BF16), vi, preferred_element_type=F32)
            o_cross = jnp.dot(qi, r_scr[...].astype(BF16), preferred_element_type=F32) * xi
            kv = lax.dot_general((kf * zeta).astype(BF16), vi, (((0,), (0,)), ((), ())),
                                 preferred_element_type=F32)
            r_scr[...] = g_chunk * r_scr[...] + kv
            if direction == 0:
                acc_scr[rows, :] = o_inner + o_cross
            else:
                finalize(rows, acc_scr[rows, :] + (o_inner + o_cross))
        if emit_state:
            st_ref[direction] = r_scr[...]


def _retention_core(proj, decay, gn_g, j, *, row0, n_seq, seq_len, state, rope, emit_state):
    heads = R_HEADS
    dk = proj.shape[1] // (6 * heads)
    dv = 2 * dk
    blk0 = row0 // seq_len
    dec = jnp.broadcast_to(decay[j][:, :, None, None], (2, heads, 1, LANES))
    in_specs = [pl.BlockSpec((seq_len, dk), lambda b, h: (blk0 + b, h)),
                pl.BlockSpec((seq_len, dk), lambda b, h: (blk0 + b, heads + h)),
                pl.BlockSpec((seq_len, dv), lambda b, h: (blk0 + b, heads + h)),
                pl.BlockSpec((seq_len, dv), lambda b, h: (blk0 + b, 2 * heads + h)),
                pl.BlockSpec((2, None, 1, LANES), lambda b, h: (0, h, 0, 0)),
                _lead((j,), (1, dv), lambda b, h: (0, h))]
    args = [proj, proj, proj, proj, dec, gn_g[:, None, :]]
    if rope is not None:
        in_specs += [pl.BlockSpec((seq_len, dk), lambda b, h: (0, 0))] * 2
        args += list(rope)
    if state is not None:
        in_specs.append(pl.BlockSpec((None, None, 2, None, dk, dv),
                                     lambda b, h: (b, j, 0, h, 0, 0)))
        args.append(state)
    out_specs = [pl.BlockSpec((seq_len, dv), lambda b, h: (b, h))]
    out_shape = [jax.ShapeDtypeStruct((n_seq * seq_len, heads * dv), BF16)]
    if emit_state:
        out_specs.append(pl.BlockSpec((None, 2, None, dk, dv), lambda b, h: (b, 0, h, 0, 0)))
        out_shape.append(jax.ShapeDtypeStruct((n_seq, 2, heads, dk, dv), F32))
    body = functools.partial(_ret_kernel, seq_len=seq_len, dk=dk, has_state=state is not None,
                             has_rope=rope is not None, emit_state=emit_state)
    return pl.pallas_call(
        body,
        grid=(n_seq, heads),
        in_specs=in_specs,
        out_specs=out_specs,
        out_shape=out_shape,
        scratch_shapes=[pltpu.VMEM((seq_len, dk), BF16), pltpu.VMEM((seq_len, dk), F32),
                        pltpu.VMEM((seq_len, dv), F32), pltpu.VMEM((dk, dv), F32)],
        compiler_params=_params(2),
        name="retention_core",
    )(*args)


def _rope_tables(seq_len, dk):
    nf = dk // 4
    inv = ROPE_BASE ** (-jnp.arange(nf, dtype=F32) / nf)
    t = jnp.arange(seq_len)
    tabs = []
    for p in (t // GRID_W, t % GRID_W):
        ang = p.astype(F32)[:, None] * inv
        tabs.append((jnp.cos(ang), jnp.sin(ang)))
    cos = jnp.concatenate([tabs[0][0], tabs[0][0], tabs[1][0], tabs[1][0]], axis=1)
    sin = jnp.concatenate([-tabs[0][1], tabs[0][1], -tabs[1][1], tabs[1][1]], axis=1)
    return cos, sin


def _lru_kernel(*refs, seq_len, has_state):
    it = iter(refs)
    y_ref, x_ref, cw_ref, cb_ref, gw_ref, gb_ref, lam_ref = (next(it) for _ in range(7))
    h0_ref = next(it) if has_state else None
    o_ref, st_ref = next(it), next(it)
    n = seq_len
    x = x_ref[...].astype(F32)
    row = lax.broadcasted_iota(jnp.int32, x.shape, 0)

    def shifted(v, d, fill):
        if d == 0:
            return v
        if d > 0:
            return jnp.where(row >= d, pltpu.roll(v, d, 0), fill)
        return jnp.where(row < n + d, pltpu.roll(v, n + d, 0), fill)

    conv_w = cw_ref.shape[0]
    pad_l = conv_w // 2
    xc = shifted(x, pad_l, 0.0) * cw_ref[0:1, :]
    for t in range(1, conv_w):
        xc = xc + shifted(x, pad_l - t, 0.0) * cw_ref[t:t + 1, :]
    xc = xc + cb_ref[...]
    xcb = xc.astype(BF16)

    h_sum = None
    for direction in range(2):
        gates = [jax.nn.sigmoid(
            jnp.dot(xcb, gw_ref[direction, gi].astype(BF16), preferred_element_type=F32)
            + gb_ref[direction, gi:gi + 1, :]) for gi in range(2)]
        log_a = (LRU_C * gates[0]) * _log_sigmoid(lam_ref[direction:direction + 1, :])
        a = jnp.exp(log_a)
        b = (jnp.sqrt(-jnp.tanh(log_a) * (a * a + 1.0)) * gates[1]) * xc
        sign = 1 if direction == 0 else -1
        d = 1
        while d < n:
            b = a * shifted(b, sign * d, 0.0) + b
            a = a * shifted(a, sign * d, 1.0)
            d *= 2
        if has_state:
            b = b + a * h0_ref[direction:direction + 1, :]
        last = n - 1 if direction == 0 else 0
        st_ref[direction:direction + 1, :] = b[last:last + 1, :]
        h_sum = b if h_sum is None else h_sum + b
    o_ref[...] = (h_sum * jax.nn.gelu(y_ref[...].astype(F32))).astype(o_ref.dtype)


def _lru_core(proj, conv_w, conv_b, gate_w, gate_b, lam, j, *, row0, n_seq, seq_len, state):
    width = proj.shape[1] // 2
    nb = width // LRU_BLOCK
    blk0 = row0 // seq_len
    in_specs = [pl.BlockSpec((seq_len, LRU_BLOCK), lambda b, c: (blk0 + b, c)),
                pl.BlockSpec((seq_len, LRU_BLOCK), lambda b, c: (blk0 + b, nb + c)),
                _lead((j,), (conv_w.shape[1], LRU_BLOCK), lambda b, c: (0, c)),
                _lead((j,), (1, LRU_BLOCK), lambda b, c: (0, c)),
                _lead((j,), (2, 2, None, LRU_BLOCK, LRU_BLOCK), lambda b, c: (0, 0, c, 0, 0)),
                _lead((j,), (2, 2, LRU_BLOCK), lambda b, c: (0, 0, c)),
                _lead((j,), (2, LRU_BLOCK), lambda b, c: (0, c))]
    args = [proj, proj, conv_w, conv_b[:, None, :], gate_w, gate_b, lam]
    if state is not None:
        in_specs.append(pl.BlockSpec((None, None, 2, LRU_BLOCK), lambda b, c: (b, j, 0, c)))
        args.append(state)
    body = functools.partial(_lru_kernel, seq_len=seq_len, has_state=state is not None)
    return pl.pallas_call(
        body,
        grid=(n_seq, nb),
        in_specs=in_specs,
        out_specs=[pl.BlockSpec((seq_len, LRU_BLOCK), lambda b, c: (b, c)),
                   pl.BlockSpec((None, 2, LRU_BLOCK), lambda b, c: (b, 0, c))],
        out_shape=[jax.ShapeDtypeStruct((n_seq * seq_len, width), BF16),
                   jax.ShapeDtypeStruct((n_seq, 2, width), F32)],
        compiler_params=_params(2),
        name="lru_core",
    )(*args)


def _norm_kernel(x_ref, g_ref, o_ref):
    x = x_ref[...]
    o_ref[...] = (x * lax.rsqrt(jnp.mean(x * x, axis=-1, keepdims=True) + EPS)) * g_ref[...]


def _final_norm(x, gain, row0, rows, tr=512):
    d = x.shape[1]
    blk0 = row0 // tr
    return pl.pallas_call(
        _norm_kernel,
        grid=(rows // tr,),
        in_specs=[pl.BlockSpec((tr, d), lambda i: (blk0 + i, 0)),
                  pl.BlockSpec((1, d), lambda i: (0, 0))],
        out_specs=pl.BlockSpec((tr, d), lambda i: (i, 0)),
        out_shape=jax.ShapeDtypeStruct((rows, d), F32),
        compiler_params=_params(1),
        name="final_norm",
    )(x, gain.reshape(1, d))


def kernel(x_prompt, x_sample, state_ret, state_lru, c, c_ctx, norm_g, final_g, w_mod, b_mod,
           ffn_w_in, ffn_w_out, a_w_in, a_v_g, a_w_s, a_b_s, a_w_out, r_w_in, r_decay, r_gn_g,
           r_w_out, l_w_in, l_conv_w, l_conv_b, l_gate_w, l_gate_b, l_lambda, l_w_out):
    batch, seq, d = x_prompt.shape
    dec_batch, dec_seq, _ = x_sample.shape
    depth = w_mod.shape[0]
    m_prompt, m_sample = batch * seq, dec_batch * dec_seq
    assert m_prompt % TM == 0 and dec_seq % TM == 0 and dec_batch + 1 <= MOD_ROWS
    prompt_tiles, tiles_per_sample = m_prompt // TM, dec_seq // TM

    def seg(i):
        return jnp.maximum(i - prompt_tiles, -1) // tiles_per_sample + 1

    x = jnp.concatenate([x_prompt.reshape(m_prompt, d), x_sample.reshape(m_sample, d)], axis=0)
    cond = jnp.concatenate(
        [c_ctx[None, :], c, jnp.zeros((MOD_ROWS - 1 - dec_batch, d), F32)], axis=0)
    mod_all = _modulation(cond, w_mod, b_mod).reshape(depth, MOD_ROWS, N_MOD, d)
    rope = _rope_tables(dec_seq, d // R_HEADS)

    def ffn(x, l, s, mod):
        hid = _project(x, norm_g[l, 2 * s], mod, ffn_w_in, (l, s), seg, shift_idx=6 * s,
                       scale_idx=6 * s + 1, act="swiglu", tn=512)
        return _out_project(hid, ffn_w_out, (l, s), x, mod, seg, gate_idx=6 * s + 2, coef=0.5)

    ret_states, lru_states = [], []
    for l in range(depth):
        kind, j = l % N_MIXERS, l // N_MIXERS
        mod = mod_all[l]
        x = ffn(x, l, 0, mod)
        pre = dict(shift_idx=3, scale_idx=4, tn=1024)
        if kind == 0:
            uv = _project(x, norm_g[l, 1], mod, a_w_in, (j,), seg, act="gelu", **pre)
            mixed = _gmlp_core(uv, a_v_g, a_w_s, a_b_s, j)
            w_out = a_w_out
        elif kind == 1:
            proj = _project(x, norm_g[l, 1], mod, r_w_in, (j,), seg, act="none", **pre)
            o_p, st = _retention_core(proj, r_decay, r_gn_g, j, row0=0, n_seq=batch,
                                      seq_len=seq, state=None, rope=None, emit_state=True)
            (o_s,) = _retention_core(proj, r_decay, r_gn_g, j, row0=m_prompt, n_seq=dec_batch,
                                     seq_len=dec_seq, state=state_ret, rope=rope,
                                     emit_state=False)
            mixed = jnp.concatenate([o_p, o_s], axis=0)
            ret_states.append(st)
            w_out = r_w_out
        else:
            proj = _project(x, norm_g[l, 1], mod, l_w_in, (j,), seg, act="none", **pre)
            o_p, st = _lru_core(proj, l_conv_w, l_conv_b, l_gate_w, l_gate_b, l_lambda, j, row0=0,
                                n_seq=batch, seq_len=seq, state=None)
            o_s, _ = _lru_core(proj, l_conv_w, l_conv_b, l_gate_w, l_gate_b, l_lambda, j,
                               row0=m_prompt, n_seq=dec_batch, seq_len=dec_seq, state=state_lru)
            mixed = jnp.concatenate([o_p, o_s], axis=0)
            lru_states.append(st)
            w_out = l_w_out
        x = _out_project(mixed, w_out, (j,), x, mod, seg, gate_idx=5, coef=1.0)
        x = ffn(x, l, 1, mod)

    y_prompt = _final_norm(x, final_g, 0, m_prompt).reshape(batch, seq, d)
    y_sample = _final_norm(x, final_g, m_prompt, m_sample).reshape(dec_batch, dec_seq, d)
    new_state_ret = jnp.stack(ret_states, axis=1)
    new_state_lru = jnp.stack(lru_states, axis=1)
    return (y_prompt, y_sample, new_state_ret, new_state_lru)
```

```python
import functools

import jax
import jax.numpy as jnp
from jax import lax
from jax.experimental import pallas as pl
from jax.experimental.pallas import tpu as pltpu

F32 = jnp.float32
BF16 = jnp.bfloat16

EPS = 1e-6
N_MIXERS = 3
N_MOD = 9
GRID_W = 64
A_CHUNK = 128
A_GROUPS = 8
R_HEADS = 8
R_CHUNK = 128
ROPE_BASE = 10000.0
LRU_BLOCK = 256
LRU_C = 8.0

LANES = 128
SUBLANES = 8
SEG_PAD = 4
SCAN_UNROLL = 8
MOD_ROWS = SUBLANES
TM = 1024
ROW_CHUNK = 32
VMEM_LIMIT = 56 * 1024 * 1024


def _params(n_grid):
    return pltpu.CompilerParams(dimension_semantics=("arbitrary",) * n_grid,
                                vmem_limit_bytes=VMEM_LIMIT)


def _lead(lead, block, index_map):
    return pl.BlockSpec((None,) * len(lead) + tuple(block),
                        lambda *g: tuple(lead) + tuple(index_map(*g)))


def _log_sigmoid(x):
    return jnp.minimum(x, 0.0) - jnp.log1p(jnp.exp(-jnp.abs(x)))


def _mod_kernel(c_ref, w_ref, b_ref, o_ref):
    s = jax.nn.silu(c_ref[...]).astype(BF16)
    o_ref[...] = jnp.dot(s, w_ref[...].astype(BF16), preferred_element_type=F32) + b_ref[...]


def _modulation(cond, w_mod, b_mod, tn=1024):
    depth, d, n = w_mod.shape
    return pl.pallas_call(
        _mod_kernel,
        grid=(depth, n // tn),
        in_specs=[pl.BlockSpec((MOD_ROWS, d), lambda l, j: (0, 0)),
                  pl.BlockSpec((None, d, tn), lambda l, j: (l, 0, j)),
                  pl.BlockSpec((None, 1, tn), lambda l, j: (l, 0, j))],
        out_specs=pl.BlockSpec((None, MOD_ROWS, tn), lambda l, j: (l, 0, j)),
        out_shape=jax.ShapeDtypeStruct((depth, MOD_ROWS, n), F32),
        compiler_params=_params(2),
        name="modulation",
    )(cond, w_mod, b_mod.reshape(depth, 1, n))


def _proj_kernel(x_ref, g_ref, mod_ref, *refs, n_w, shift_idx, scale_idx, act):
    w_refs, o_ref, h_ref = refs[:n_w], refs[n_w], refs[n_w + 1]

    @pl.when(pl.program_id(1) == 0)
    def _prologue():
        gain = g_ref[...]
        scale = 1.0 + mod_ref[scale_idx:scale_idx + 1, :]
        shift = mod_ref[shift_idx:shift_idx + 1, :]

        def rows_body(r, carry):
            rows = pl.ds(pl.multiple_of(r * ROW_CHUNK, ROW_CHUNK), ROW_CHUNK)
            x = x_ref[rows, :]
            y = x * lax.rsqrt(jnp.mean(x * x, axis=-1, keepdims=True) + EPS)
            h_ref[rows, :] = ((y * gain) * scale + shift).astype(h_ref.dtype)
            return carry

        lax.fori_loop(0, x_ref.shape[0] // ROW_CHUNK, rows_body, 0, unroll=4)

    h = h_ref[...]
    ys = [jnp.dot(h, w[...].astype(BF16), preferred_element_type=F32) for w in w_refs]
    if act == "swiglu":
        out = jax.nn.silu(ys[0]) * ys[1]
    elif act == "gelu":
        out = jax.nn.gelu(ys[0])
    else:
        out = ys[0]
    o_ref[...] = out.astype(o_ref.dtype)


def _project(x, gain, mod, w, lead, seg, *, shift_idx, scale_idx, act, tn):
    m, d = x.shape
    n_total = w.shape[-1]
    if act == "swiglu":
        n_out = n_total // 2
        half = n_out // tn
        w_specs = [_lead(lead, (d, tn), lambda i, j: (0, j)),
                   _lead(lead, (d, tn), lambda i, j: (0, j + half))]
    else:
        n_out = n_total
        w_specs = [_lead(lead, (d, tn), lambda i, j: (0, j))]
    body = functools.partial(_proj_kernel, n_w=len(w_specs), shift_idx=shift_idx,
                             scale_idx=scale_idx, act=act)
    return pl.pallas_call(
        body,
        grid=(m // TM, n_out // tn),
        in_specs=[pl.BlockSpec((TM, d), lambda i, j: (i, 0)),
                  pl.BlockSpec((1, d), lambda i, j: (0, 0)),
                  pl.BlockSpec((None, N_MOD, d), lambda i, j: (seg(i), 0, 0))] + w_specs,
        out_specs=pl.BlockSpec((TM, tn), lambda i, j: (i, j)),
        out_shape=jax.ShapeDtypeStruct((m, n_out), BF16),
        scratch_shapes=[pltpu.VMEM((TM, d), BF16)],
        compiler_params=_params(2),
        name="project_" + act,
    )(x, gain.reshape(1, d), mod, *([w] * len(w_specs)))


def _out_kernel(*refs, n_a, split, gate_idx, coef):
    a_refs = refs[:n_a]
    w_ref, x_ref, mod_ref, o_ref = refs[n_a:]

    def update(a_ref):
        y = jnp.dot(a_ref[...], w_ref[...].astype(BF16), preferred_element_type=F32)
        gate = mod_ref[gate_idx:gate_idx + 1, :]
        if coef != 1.0:
            gate = coef * gate
        o_ref[...] = x_ref[...] + gate * y

    if n_a == 1:
        update(a_refs[0])
    else:
        pl.when(pl.program_id(0) < split)(functools.partial(update, a_refs[0]))
        pl.when(pl.program_id(0) >= split)(functools.partial(update, a_refs[1]))


def _out_project(a, w, lead, x, mod, seg, *, gate_idx, coef, tn=256):
    m, d = x.shape
    if isinstance(a, tuple):
        split = a[0].shape[0] // TM
        k = a[0].shape[1]
        a_specs = [pl.BlockSpec((TM, k), lambda i, j: (jnp.minimum(i, split - 1), 0)),
                   pl.BlockSpec((TM, k), lambda i, j: (jnp.maximum(i - split, 0), 0))]
    else:
        split, k = 0, a.shape[1]
        a_specs = [pl.BlockSpec((TM, k), lambda i, j: (i, 0))]
        a = (a,)
    body = functools.partial(_out_kernel, n_a=len(a), split=split, gate_idx=gate_idx, coef=coef)
    return pl.pallas_call(
        body,
        grid=(m // TM, d // tn),
        in_specs=a_specs + [_lead(lead, (k, tn), lambda i, j: (0, j)),
                            pl.BlockSpec((TM, tn), lambda i, j: (i, j)),
                            pl.BlockSpec((None, N_MOD, tn), lambda i, j: (seg(i), 0, j))],
        out_specs=pl.BlockSpec((TM, tn), lambda i, j: (i, j)),
        out_shape=jax.ShapeDtypeStruct((m, d), F32),
        compiler_params=_params(2),
        name="out_project",
    )(*a, w, x, mod)


def _gmlp_kernel(u_ref, v_ref, vg_ref, ws_ref, bs_ref, o_ref):
    width = v_ref.shape[1]
    gw = width // A_GROUPS
    vg = vg_ref[...]
    for c in range(v_ref.shape[0] // A_CHUNK):
        rows = slice(c * A_CHUNK, (c + 1) * A_CHUNK)
        v = v_ref[rows, :].astype(F32)
        vn = v * lax.rsqrt(jnp.mean(v * v, axis=-1, keepdims=True) + EPS)
        vn = (vn * vg).astype(BF16)
        for g in range(A_GROUPS):
            cols = slice(g * gw, (g + 1) * gw)
            sv = jnp.dot(ws_ref[g].astype(BF16), vn[:, cols], preferred_element_type=F32)
            sv = sv + bs_ref[:, g:g + 1]
            o_ref[rows, cols] = (u_ref[rows, cols].astype(F32) * sv).astype(o_ref.dtype)


def _gmlp_core(uv, v_g, w_s, b_s, j, rows_per_step=2 * A_CHUNK):
    m, two_w = uv.shape
    width = two_w // 2
    tr = rows_per_step
    return pl.pallas_call(
        _gmlp_kernel,
        grid=(m // tr,),
        in_specs=[pl.BlockSpec((tr, width), lambda i: (i, 0)),
                  pl.BlockSpec((tr, width), lambda i: (i, 1)),
                  _lead((j,), (1, width), lambda i: (0, 0)),
                  _lead((j,), (A_GROUPS, A_CHUNK, A_CHUNK), lambda i: (0, 0, 0)),
                  _lead((j,), (A_CHUNK, A_GROUPS), lambda i: (0, 0))],
        out_specs=pl.BlockSpec((tr, width), lambda i: (i, 0)),
        out_shape=jax.ShapeDtypeStruct((m, width), BF16),
        compiler_params=_params(1),
        name="gmlp_core",
    )(uv, uv, v_g[:, None, :], w_s, jnp.swapaxes(b_s, 1, 2))


def _ret_kernel(*refs, seq_len, dk, hb, has_state, has_rope, emit_state):
    it = iter(refs)
    q_ref, k_ref, v_ref, g_ref, dec_ref, gn_ref = (next(it) for _ in range(6))
    cos_ref, sin_ref = (next(it), next(it)) if has_rope else (None, None)
    s0_ref = next(it) if has_state else None
    o_ref = next(it)
    st_ref = next(it) if emit_state else None
    q_scr, k_scr, acc_scr, r_scr = (next(it) for _ in range(4))
    chunk = R_CHUNK
    n_chunks = seq_len // chunk
    dv = 2 * dk

    ri = lax.broadcasted_iota(jnp.int32, (chunk, chunk), 0).astype(F32)
    ci = lax.broadcasted_iota(jnp.int32, (chunk, chunk), 1).astype(F32)
    rel = ri - ci

    def head(hh):
        qk_cols = slice(hh * dk, (hh + 1) * dk)
        v_cols = slice(hh * dv, (hh + 1) * dv)
        q = q_ref[:, qk_cols].astype(F32)
        k = k_ref[:, qk_cols].astype(F32)
        if has_rope:
            cos, sin = cos_ref[...], sin_ref[...]

            def rope(t):
                swapped = jnp.concatenate(
                    [pltpu.roll(t[:, s:s + LANES], LANES // 2, 1) for s in range(0, dk, LANES)],
                    axis=1)
                return t * cos + swapped * sin

            q, k = rope(q), rope(k)
        q_scr[:, qk_cols] = q.astype(BF16)
        k_scr[:, qk_cols] = k * (dk ** -0.5)

        consts = []
        for direction in range(2):
            lg = _log_sigmoid(dec_ref[direction, hh])
            if direction == 0:
                dmat = jnp.where(rel >= 0, jnp.exp(jnp.maximum(rel, 0.0) * lg), 0.0)
                zeta = jnp.exp((chunk - 1 - ri) * lg)[:, :1]
                xi = jnp.exp((ri + 1) * lg)[:, :1]
            else:
                dmat = jnp.where(rel <= 0, jnp.exp(jnp.maximum(-rel, 0.0) * lg), 0.0)
                zeta = jnp.exp(ri * lg)[:, :1]
                xi = jnp.exp((chunk - ri) * lg)[:, :1]
            consts.append((dmat, zeta, xi, jnp.exp(chunk * lg)[:, :1]))
            if has_state:
                r_scr[direction, hh] = s0_ref[direction, hh].astype(F32)
            else:
                r_scr[direction, hh] = jnp.zeros((dk, dv), F32)

        def chunk_step(direction, i):
            dmat, zeta, xi, g_chunk = consts[direction]
            rows = slice(i * chunk, (i + 1) * chunk)
            qi = q_scr[rows, qk_cols]
            kf = k_scr[rows, qk_cols]
            vi = v_ref[rows, v_cols]
            s = lax.dot_general(qi, kf.astype(BF16), (((1,), (1,)), ((), ())),
                                preferred_element_type=F32)
            o_inner = jnp.dot((s * dmat).astype(BF16), vi, preferred_element_type=F32)
            o_cross = jnp.dot(qi, r_scr[direction, hh].astype(BF16),
                              preferred_element_type=F32) * xi
            kv = lax.dot_general((kf * zeta).astype(BF16), vi, (((0,), (0,)), ((), ())),
                                 preferred_element_type=F32)
            r_scr[direction, hh] = g_chunk * r_scr[direction, hh] + kv
            acc_scr[direction, rows, v_cols] = o_inner + o_cross

        for step in range(n_chunks):
            chunk_step(0, step)
            chunk_step(1, n_chunks - 1 - step)

        gn = gn_ref[:, v_cols]
        for i in range(n_chunks):
            rows = slice(i * chunk, (i + 1) * chunk)
            o = acc_scr[0, rows, v_cols] + acc_scr[1, rows, v_cols]
            mu = jnp.mean(o, axis=-1, keepdims=True)
            var = jnp.mean(jnp.square(o - mu), axis=-1, keepdims=True)
            on = ((o - mu) * lax.rsqrt(var + EPS)) * gn
            o_ref[rows, v_cols] = (jax.nn.silu(g_ref[rows, v_cols].astype(F32)) * on
                                   ).astype(o_ref.dtype)

    for hh in range(hb):
        head(hh)
    if emit_state:
        st_ref[...] = r_scr[...]


def _retention_core(proj, decay, gn_g, j, *, row0, n_seq, seq_len, state, rope, emit_state,
                    hb=2):
    heads = R_HEADS
    dk = proj.shape[1] // (6 * heads)
    dv = 2 * dk
    blk0 = row0 // seq_len
    hg = heads // hb
    dec = jnp.broadcast_to(decay[j][:, :, None, None], (2, heads, 1, LANES))
    in_specs = [pl.BlockSpec((seq_len, hb * dk), lambda b, h: (blk0 + b, h)),
                pl.BlockSpec((seq_len, hb * dk), lambda b, h: (blk0 + b, hg + h)),
                pl.BlockSpec((seq_len, hb * dv), lambda b, h: (blk0 + b, hg + h)),
                pl.BlockSpec((seq_len, hb * dv), lambda b, h: (blk0 + b, 2 * hg + h)),
                pl.BlockSpec((2, hb, 1, LANES), lambda b, h: (0, h, 0, 0)),
                _lead((j,), (1, hb * dv), lambda b, h: (0, h))]
    args = [proj, proj, proj, proj, dec, gn_g[:, None, :]]
    if rope is not None:
        in_specs += [pl.BlockSpec((seq_len, dk), lambda b, h: (0, 0))] * 2
        args += list(rope)
    if state is not None:
        in_specs.append(pl.BlockSpec((None, None, 2, hb, dk, dv),
                                     lambda b, h: (b, j, 0, h, 0, 0)))
        args.append(state)
    out_specs = [pl.BlockSpec((seq_len, hb * dv), lambda b, h: (b, h))]
    out_shape = [jax.ShapeDtypeStruct((n_seq * seq_len, heads * dv), BF16)]
    if emit_state:
        out_specs.append(pl.BlockSpec((None, 2, hb, dk, dv), lambda b, h: (b, 0, h, 0, 0)))
        out_shape.append(jax.ShapeDtypeStruct((n_seq, 2, heads, dk, dv), F32))
    body = functools.partial(_ret_kernel, seq_len=seq_len, dk=dk, hb=hb,
                             has_state=state is not None, has_rope=rope is not None,
                             emit_state=emit_state)
    return pl.pallas_call(
        body,
        grid=(n_seq, hg),
        in_specs=in_specs,
        out_specs=out_specs,
        out_shape=out_shape,
        scratch_shapes=[pltpu.VMEM((seq_len, hb * dk), BF16),
                        pltpu.VMEM((seq_len, hb * dk), F32),
                        pltpu.VMEM((2, seq_len, hb * dv), F32),
                        pltpu.VMEM((2, hb, dk, dv), F32)],
        compiler_params=_params(2),
        name="retention_core",
    )(*args)


def _rope_tables(seq_len, dk):
    nf = dk // 4
    inv = ROPE_BASE ** (-jnp.arange(nf, dtype=F32) / nf)
    t = jnp.arange(seq_len)
    tabs = []
    for p in (t // GRID_W, t % GRID_W):
        ang = p.astype(F32)[:, None] * inv
        tabs.append((jnp.cos(ang), jnp.sin(ang)))
    cos = jnp.concatenate([tabs[0][0], tabs[0][0], tabs[1][0], tabs[1][0]], axis=1)
    sin = jnp.concatenate([-tabs[0][1], tabs[0][1], -tabs[1][1], tabs[1][1]], axis=1)
    return cos, sin


def _lru_kernel(*refs, seq_len, has_state):
    it = iter(refs)
    y_ref, x_ref, cw_ref, cb_ref, gw_ref, gb_ref, lam_ref = (next(it) for _ in range(7))
    h0_ref = next(it) if has_state else None
    o_ref, st_ref, ab_scr, h_scr = (next(it) for _ in range(4))
    n = seq_len
    x = x_ref[...].astype(F32)
    row = lax.broadcasted_iota(jnp.int32, x.shape, 0)

    def shifted(v, d, fill):
        if d == 0:
            return v
        if d > 0:
            return jnp.where(row >= d, pltpu.roll(v, d, 0), fill)
        return jnp.where(row < n + d, pltpu.roll(v, n + d, 0), fill)

    conv_w = cw_ref.shape[0]
    pad_l = conv_w // 2
    xc = shifted(x, pad_l, 0.0) * cw_ref[0:1, :]
    for t in range(1, conv_w):
        xc = xc + shifted(x, pad_l - t, 0.0) * cw_ref[t:t + 1, :]
    xc = xc + cb_ref[...]
    xcb = xc.astype(BF16)

    seg = n // SUBLANES
    pitch = seg + SEG_PAD
    n_slabs = x.shape[1] // LANES
    for direction in range(2):
        gates = [jax.nn.sigmoid(
            jnp.dot(xcb, gw_ref[direction, gi].astype(BF16), preferred_element_type=F32)
            + gb_ref[direction, gi:gi + 1, :]) for gi in range(2)]
        log_a = (LRU_C * gates[0]) * _log_sigmoid(lam_ref[direction:direction + 1, :])
        a = jnp.exp(log_a)
        b = (jnp.sqrt(-jnp.tanh(log_a) * (a * a + 1.0)) * gates[1]) * xc
        for sl in range(n_slabs):
            cols = slice(sl * LANES, (sl + 1) * LANES)
            for s in range(SUBLANES):
                src, dst = slice(s * seg, (s + 1) * seg), slice(s * pitch, s * pitch + seg)
                ab_scr[direction, 0, sl, dst, :] = a[src, cols]
                ab_scr[direction, 1, sl, dst, :] = b[src, cols]

    def step_rows(i):
        return pl.ds(i, SUBLANES, stride=pitch)

    chains = [(d, sl) for d in range(2) for sl in range(n_slabs)]

    def scan_body(i, carry):
        out = []
        for (d, sl), (a_run, h_run) in zip(chains, carry):
            rows = step_rows(i if d == 0 else seg - 1 - i)
            a_i = ab_scr[d, 0, sl, rows, :]
            h_run = a_i * h_run + ab_scr[d, 1, sl, rows, :]
            a_run = a_i * a_run
            ab_scr[d, 0, sl, rows, :] = a_run
            ab_scr[d, 1, sl, rows, :] = h_run
            out.append((a_run, h_run))
        return tuple(out)

    init = tuple((jnp.ones((SUBLANES, LANES), F32), jnp.zeros((SUBLANES, LANES), F32))
                 for _ in chains)
    ends = lax.fori_loop(0, seg, scan_body, init, unroll=SCAN_UNROLL)

    entering = []
    for (d, sl), (a_end, h_end) in zip(chains, ends):
        cols = slice(sl * LANES, (sl + 1) * LANES)
        c = h0_ref[d:d + 1, cols] if has_state else jnp.zeros((1, LANES), F32)
        rows = [None] * SUBLANES
        for s in (range(SUBLANES) if d == 0 else range(SUBLANES - 1, -1, -1)):
            rows[s] = c
            c = h_end[s:s + 1, :] + a_end[s:s + 1, :] * c
        st_ref[d:d + 1, cols] = c
        entering.append(jnp.concatenate(rows, axis=0))

    def fix_body(i, carry):
        rows = step_rows(i)
        for sl in range(n_slabs):
            h_f, h_b = (ab_scr[d, 1, sl, rows, :]
                        + ab_scr[d, 0, sl, rows, :] * entering[chains.index((d, sl))]
                        for d in range(2))
            h_scr[sl, rows, :] = h_f + h_b
        return carry

    lax.fori_loop(0, seg, fix_body, 0, unroll=SCAN_UNROLL)

    for sl in range(n_slabs):
        cols = slice(sl * LANES, (sl + 1) * LANES)
        for s in range(SUBLANES):
            src, dst = slice(s * pitch, s * pitch + seg), slice(s * seg, (s + 1) * seg)
            o_ref[dst, cols] = (h_scr[sl, src, :]
                                * jax.nn.gelu(y_ref[dst, cols].astype(F32))).astype(o_ref.dtype)


def _lru_core(proj, conv_w, conv_b, gate_w, gate_b, lam, j, *, row0, n_seq, seq_len, state):
    width = proj.shape[1] // 2
    nb = width // LRU_BLOCK
    blk0 = row0 // seq_len
    in_specs = [pl.BlockSpec((seq_len, LRU_BLOCK), lambda b, c: (blk0 + b, c)),
                pl.BlockSpec((seq_len, LRU_BLOCK), lambda b, c: (blk0 + b, nb + c)),
                _lead((j,), (conv_w.shape[1], LRU_BLOCK), lambda b, c: (0, c)),
                _lead((j,), (1, LRU_BLOCK), lambda b, c: (0, c)),
                _lead((j,), (2, 2, None, LRU_BLOCK, LRU_BLOCK), lambda b, c: (0, 0, c, 0, 0)),
                _lead((j,), (2, 2, LRU_BLOCK), lambda b, c: (0, 0, c)),
                _lead((j,), (2, LRU_BLOCK), lambda b, c: (0, c))]
    args = [proj, proj, conv_w, conv_b[:, None, :], gate_w, gate_b, lam]
    if state is not None:
        in_specs.append(pl.BlockSpec((None, None, 2, LRU_BLOCK), lambda b, c: (b, j, 0, c)))
        args.append(state)
    body = functools.partial(_lru_kernel, seq_len=seq_len, has_state=state is not None)
    n_slabs = LRU_BLOCK // LANES
    scr_rows = SUBLANES * (seq_len // SUBLANES + SEG_PAD)
    return pl.pallas_call(
        body,
        grid=(n_seq, nb),
        in_specs=in_specs,
        out_specs=[pl.BlockSpec((seq_len, LRU_BLOCK), lambda b, c: (b, c)),
                   pl.BlockSpec((None, 2, LRU_BLOCK), lambda b, c: (b, 0, c))],
        out_shape=[jax.ShapeDtypeStruct((n_seq * seq_len, width), BF16),
                   jax.ShapeDtypeStruct((n_seq, 2, width), F32)],
        scratch_shapes=[pltpu.VMEM((2, 2, n_slabs, scr_rows, LANES), F32),
                        pltpu.VMEM((n_slabs, scr_rows, LANES), F32)],
        compiler_params=_params(2),
        name="lru_core",
    )(*args)


def _norm_kernel(x_ref, g_ref, o_ref):
    x = x_ref[...]
    o_ref[...] = (x * lax.rsqrt(jnp.mean(x * x, axis=-1, keepdims=True) + EPS)) * g_ref[...]


def _final_norm(x, gain, row0, rows, tr=512):
    d = x.shape[1]
    blk0 = row0 // tr
    return pl.pallas_call(
        _norm_kernel,
        grid=(rows // tr,),
        in_specs=[pl.BlockSpec((tr, d), lambda i: (blk0 + i, 0)),
                  pl.BlockSpec((1, d), lambda i: (0, 0))],
        out_specs=pl.BlockSpec((tr, d), lambda i: (i, 0)),
        out_shape=jax.ShapeDtypeStruct((rows, d), F32),
        compiler_params=_params(1),
        name="final_norm",
    )(x, gain.reshape(1, d))


def kernel(x_prompt, x_sample, state_ret, state_lru, c, c_ctx, norm_g, final_g, w_mod, b_mod,
           ffn_w_in, ffn_w_out, a_w_in, a_v_g, a_w_s, a_b_s, a_w_out, r_w_in, r_decay, r_gn_g,
           r_w_out, l_w_in, l_conv_w, l_conv_b, l_gate_w, l_gate_b, l_lambda, l_w_out):
    batch, seq, d = x_prompt.shape
    dec_batch, dec_seq, _ = x_sample.shape
    depth = w_mod.shape[0]
    m_prompt, m_sample = batch * seq, dec_batch * dec_seq
    assert m_prompt % TM == 0 and dec_seq % TM == 0 and dec_batch + 1 <= MOD_ROWS
    prompt_tiles, tiles_per_sample = m_prompt // TM, dec_seq // TM

    def seg(i):
        return jnp.maximum(i - prompt_tiles, -1) // tiles_per_sample + 1

    x = jnp.concatenate([x_prompt.reshape(m_prompt, d), x_sample.reshape(m_sample, d)], axis=0)
    cond = jnp.concatenate(
        [c_ctx[None, :], c, jnp.zeros((MOD_ROWS - 1 - dec_batch, d), F32)], axis=0)
    mod_all = _modulation(cond, w_mod, b_mod).reshape(depth, MOD_ROWS, N_MOD, d)
    rope = _rope_tables(dec_seq, d // R_HEADS)

    def ffn(x, l, s, mod):
        hid = _project(x, norm_g[l, 2 * s], mod, ffn_w_in, (l, s), seg, shift_idx=6 * s,
                       scale_idx=6 * s + 1, act="swiglu", tn=512)
        return _out_project(hid, ffn_w_out, (l, s), x, mod, seg, gate_idx=6 * s + 2, coef=0.5)

    ret_states, lru_states = [], []
    for l in range(depth):
        kind, j = l % N_MIXERS, l // N_MIXERS
        mod = mod_all[l]
        x = ffn(x, l, 0, mod)
        pre = dict(shift_idx=3, scale_idx=4, tn=1024)
        if kind == 0:
            uv = _project(x, norm_g[l, 1], mod, a_w_in, (j,), seg, act="gelu", **pre)
            mixed = _gmlp_core(uv, a_v_g, a_w_s, a_b_s, j)
            w_out = a_w_out
        elif kind == 1:
            proj = _project(x, norm_g[l, 1], mod, r_w_in, (j,), seg, act="none", **pre)
            o_p, st = _retention_core(proj, r_decay, r_gn_g, j, row0=0, n_seq=batch,
                                      seq_len=seq, state=None, rope=None, emit_state=True,
                                      hb=4)
            (o_s,) = _retention_core(proj, r_decay, r_gn_g, j, row0=m_prompt, n_seq=dec_batch,
                                     seq_len=dec_seq, state=state_ret, rope=rope,
                                     emit_state=False)
            mixed = (o_p, o_s)
            ret_states.append(st)
            w_out = r_w_out
        else:
            proj = _project(x, norm_g[l, 1], mod, l_w_in, (j,), seg, act="none", **pre)
            o_p, st = _lru_core(proj, l_conv_w, l_conv_b, l_gate_w, l_gate_b, l_lambda, j, row0=0,
                                n_seq=batch, seq_len=seq, state=None)
            o_s, _ = _lru_core(proj, l_conv_w, l_conv_b, l_gate_w, l_gate_b, l_lambda, j,
                               row0=m_prompt, n_seq=dec_batch, seq_len=dec_seq, state=state_lru)
            mixed = (o_p, o_s)
            lru_states.append(st)
            w_out = l_w_out
        x = _out_project(mixed, w_out, (j,), x, mod, seg, gate_idx=5, coef=1.0)
        x = ffn(x, l, 1, mod)

    y_prompt = _final_norm(x, final_g, 0, m_prompt).reshape(batch, seq, d)
    y_sample = _final_norm(x, final_g, m_prompt, m_sample).reshape(dec_batch, dec_seq, d)
    new_state_ret = jnp.stack(ret_states, axis=1)
    new_state_lru = jnp.stack(lru_states, axis=1)
    return (y_prompt, y_sample, new_state_ret, new_state_lru)
```

```python
import functools

import jax
import jax.numpy as jnp
from jax import lax
from jax.experimental import pallas as pl
from jax.experimental.pallas import tpu as pltpu

F32 = jnp.float32
BF16 = jnp.bfloat16

EPS = 1e-6
N_MIXERS = 3
N_MOD = 9
GRID_W = 64
A_CHUNK = 128
A_GROUPS = 8
R_HEADS = 8
R_CHUNK = 128
ROPE_BASE = 10000.0
LRU_BLOCK = 256
LRU_C = 8.0

LANES = 128
SUBLANES = 8
MXU_N = 256
SEG_PAD = 4
SCAN_UNROLL = 8
MOD_ROWS = SUBLANES
TM = 1024
ROW_CHUNK = 32
CAST_ROWS = 256
VMEM_LIMIT = 56 * 1024 * 1024


def _params(n_grid):
    return pltpu.CompilerParams(dimension_semantics=("arbitrary",) * n_grid,
                                vmem_limit_bytes=VMEM_LIMIT)


def _lead(lead, block, index_map):
    return pl.BlockSpec((None,) * len(lead) + tuple(block),
                        lambda *g: tuple(lead) + tuple(index_map(*g)))


def _log_sigmoid(x):
    return jnp.minimum(x, 0.0) - jnp.log1p(jnp.exp(-jnp.abs(x)))


def _mod_kernel(c_ref, w_ref, b_ref, o_ref):
    s = jax.nn.silu(c_ref[...]).astype(BF16)
    o_ref[...] = jnp.dot(s, w_ref[...].astype(BF16), preferred_element_type=F32) + b_ref[...]


def _modulation(cond, w_mod, b_mod, tn=1024):
    depth, d, n = w_mod.shape
    return pl.pallas_call(
        _mod_kernel,
        grid=(depth, n // tn),
        in_specs=[pl.BlockSpec((MOD_ROWS, d), lambda l, j: (0, 0)),
                  pl.BlockSpec((None, d, tn), lambda l, j: (l, 0, j)),
                  pl.BlockSpec((None, 1, tn), lambda l, j: (l, 0, j))],
        out_specs=pl.BlockSpec((None, MOD_ROWS, tn), lambda l, j: (l, 0, j)),
        out_shape=jax.ShapeDtypeStruct((depth, MOD_ROWS, n), F32),
        compiler_params=_params(2),
        name="modulation",
    )(cond, w_mod, b_mod.reshape(depth, 1, n))


def _proj_kernel(x_ref, g_ref, mod_ref, *refs, n_w, shift_idx, scale_idx, act, n_cast):
    w_refs = refs[:n_w]
    if n_cast:
        wc_ref, o_ref, wb_ref, h_ref = refs[n_w:]
    else:
        o_ref, h_ref = refs[n_w:]

    @pl.when(pl.program_id(1) == 0)
    def _prologue():
        gain = g_ref[...]
        scale = 1.0 + mod_ref[scale_idx:scale_idx + 1, :]
        shift = mod_ref[shift_idx:shift_idx + 1, :]

        def rows_body(r, carry):
            rows = pl.ds(pl.multiple_of(r * ROW_CHUNK, ROW_CHUNK), ROW_CHUNK)
            x = x_ref[rows, :]
            y = x * lax.rsqrt(jnp.mean(x * x, axis=-1, keepdims=True) + EPS)
            h_ref[rows, :] = ((y * gain) * scale + shift).astype(h_ref.dtype)
            return carry

        lax.fori_loop(0, x_ref.shape[0] // ROW_CHUNK, rows_body, 0, unroll=4)

    if n_cast:
        wb_ref[...] = wc_ref[...].astype(BF16)

    h = h_ref[...]
    for c in range(0, o_ref.shape[1], MXU_N):
        cols = slice(c, c + MXU_N)
        ys = [jnp.dot(h, w[:, cols].astype(BF16), preferred_element_type=F32) for w in w_refs]
        if act == "swiglu":
            out = jax.nn.silu(ys[0]) * ys[1]
        elif act == "gelu":
            out = jax.nn.gelu(ys[0])
        else:
            out = ys[0]
        o_ref[:, cols] = out.astype(o_ref.dtype)


def _project(x, gain, mod, w, lead, seg, *, shift_idx, scale_idx, act, tn, cast=None):
    m, d = x.shape
    n_total = w.shape[-1]
    if act == "swiglu":
        n_out = n_total // 2
        half = n_out // tn
        w_specs = [_lead(lead, (d, tn), lambda i, j: (0, j)),
                   _lead(lead, (d, tn), lambda i, j: (0, j + half))]
    else:
        n_out = n_total
        w_specs = [_lead(lead, (d, tn), lambda i, j: (0, j))]
    grid = (m // TM, n_out // tn)
    in_specs = [pl.BlockSpec((TM, d), lambda i, j: (i, 0)),
                pl.BlockSpec((1, d), lambda i, j: (0, 0)),
                pl.BlockSpec((None, N_MOD, d), lambda i, j: (seg(i), 0, 0))] + w_specs
    args = [x, gain.reshape(1, d), mod] + [w] * len(w_specs)
    out_specs = [pl.BlockSpec((TM, tn), lambda i, j: (i, j))]
    out_shape = [jax.ShapeDtypeStruct((m, n_out), BF16)]
    n_cast = 0
    if cast is not None:
        w_next, lead_next = cast
        k_next, d_next = w_next.shape[-2:]
        n_cast = k_next // CAST_ROWS
        assert k_next % CAST_ROWS == 0 and n_cast <= grid[0] * grid[1]

        def chunk(i, j):
            return (jnp.minimum(i * grid[1] + j, n_cast - 1), 0)

        in_specs.append(_lead(lead_next, (CAST_ROWS, d_next), chunk))
        args.append(w_next)
        out_specs.append(pl.BlockSpec((CAST_ROWS, d_next), chunk))
        out_shape.append(jax.ShapeDtypeStruct((k_next, d_next), BF16))
    body = functools.partial(_proj_kernel, n_w=len(w_specs), shift_idx=shift_idx,
                             scale_idx=scale_idx, act=act, n_cast=n_cast)
    outs = pl.pallas_call(
        body,
        grid=grid,
        in_specs=in_specs,
        out_specs=out_specs,
        out_shape=out_shape,
        scratch_shapes=[pltpu.VMEM((TM, d), BF16)],
        compiler_params=_params(2),
        name="project_" + act,
    )(*args)
    return outs if cast is not None else outs[0]


def _out_kernel(*refs, n_a, split, gate_idx, coef):
    a_refs = refs[:n_a]
    w_ref, x_ref, mod_ref, o_ref = refs[n_a:]

    def update(a_ref):
        y = jnp.dot(a_ref[...], w_ref[...], preferred_element_type=F32)
        gate = mod_ref[gate_idx:gate_idx + 1, :]
        if coef != 1.0:
            gate = coef * gate
        o_ref[...] = x_ref[...] + gate * y

    if n_a == 1:
        update(a_refs[0])
    else:
        pl.when(pl.program_id(0) < split)(functools.partial(update, a_refs[0]))
        pl.when(pl.program_id(0) >= split)(functools.partial(update, a_refs[1]))


def _out_project(a, w, x, mod, seg, *, gate_idx, coef, tn=512):
    m, d = x.shape
    assert w.dtype == BF16
    if isinstance(a, tuple):
        split = a[0].shape[0] // TM
        k = a[0].shape[1]
        a_specs = [pl.BlockSpec((TM, k), lambda i, j: (jnp.minimum(i, split - 1), 0)),
                   pl.BlockSpec((TM, k), lambda i, j: (jnp.maximum(i - split, 0), 0))]
    else:
        split, k = 0, a.shape[1]
        a_specs = [pl.BlockSpec((TM, k), lambda i, j: (i, 0))]
        a = (a,)
    body = functools.partial(_out_kernel, n_a=len(a), split=split, gate_idx=gate_idx, coef=coef)
    return pl.pallas_call(
        body,
        grid=(m // TM, d // tn),
        in_specs=a_specs + [pl.BlockSpec((k, tn), lambda i, j: (0, j)),
                            pl.BlockSpec((TM, tn), lambda i, j: (i, j)),
                            pl.BlockSpec((None, N_MOD, tn), lambda i, j: (seg(i), 0, j))],
        out_specs=pl.BlockSpec((TM, tn), lambda i, j: (i, j)),
        out_shape=jax.ShapeDtypeStruct((m, d), F32),
        compiler_params=_params(2),
        name="out_project",
    )(*a, w, x, mod)


def _gmlp_kernel(u_ref, v_ref, vg_ref, ws_ref, bs_ref, o_ref):
    width = v_ref.shape[1]
    gw = width // A_GROUPS
    vg = vg_ref[...]
    for c in range(v_ref.shape[0] // A_CHUNK):
        rows = slice(c * A_CHUNK, (c + 1) * A_CHUNK)
        v = v_ref[rows, :].astype(F32)
        vn = v * lax.rsqrt(jnp.mean(v * v, axis=-1, keepdims=True) + EPS)
        vn = (vn * vg).astype(BF16)
        for g in range(A_GROUPS):
            cols = slice(g * gw, (g + 1) * gw)
            sv = jnp.dot(ws_ref[g].astype(BF16), vn[:, cols], preferred_element_type=F32)
            sv = sv + bs_ref[:, g:g + 1]
            o_ref[rows, cols] = (u_ref[rows, cols].astype(F32) * sv).astype(o_ref.dtype)


def _gmlp_core(uv, v_g, w_s, b_s, j, rows_per_step=2 * A_CHUNK):
    m, two_w = uv.shape
    width = two_w // 2
    tr = rows_per_step
    return pl.pallas_call(
        _gmlp_kernel,
        grid=(m // tr,),
        in_specs=[pl.BlockSpec((tr, width), lambda i: (i, 0)),
                  pl.BlockSpec((tr, width), lambda i: (i, 1)),
                  _lead((j,), (1, width), lambda i: (0, 0)),
                  _lead((j,), (A_GROUPS, A_CHUNK, A_CHUNK), lambda i: (0, 0, 0)),
                  _lead((j,), (A_CHUNK, A_GROUPS), lambda i: (0, 0))],
        out_specs=pl.BlockSpec((tr, width), lambda i: (i, 0)),
        out_shape=jax.ShapeDtypeStruct((m, width), BF16),
        compiler_params=_params(1),
        name="gmlp_core",
    )(uv, uv, v_g[:, None, :], w_s, jnp.swapaxes(b_s, 1, 2))


def _ret_kernel(*refs, seq_len, dk, hb, has_state, has_rope, emit_state):
    it = iter(refs)
    q_ref, k_ref, v_ref, g_ref, dec_ref, gn_ref = (next(it) for _ in range(6))
    cos_ref, sin_ref = (next(it), next(it)) if has_rope else (None, None)
    s0_ref = next(it) if has_state else None
    o_ref = next(it)
    st_ref = next(it) if emit_state else None
    q_scr, k_scr, acc_scr, r_scr = (next(it) for _ in range(4))
    chunk = R_CHUNK
    n_chunks = seq_len // chunk
    dv = 2 * dk

    ri = lax.broadcasted_iota(jnp.int32, (chunk, chunk), 0).astype(F32)
    ci = lax.broadcasted_iota(jnp.int32, (chunk, chunk), 1).astype(F32)
    rel = ri - ci

    def head(hh):
        qk_cols = slice(hh * dk, (hh + 1) * dk)
        v_cols = slice(hh * dv, (hh + 1) * dv)
        q = q_ref[:, qk_cols].astype(F32)
        k = k_ref[:, qk_cols].astype(F32)
        if has_rope:
            cos, sin = cos_ref[...], sin_ref[...]

            def rope(t):
                swapped = jnp.concatenate(
                    [pltpu.roll(t[:, s:s + LANES], LANES // 2, 1) for s in range(0, dk, LANES)],
                    axis=1)
                return t * cos + swapped * sin

            q, k = rope(q), rope(k)
        q_scr[:, qk_cols] = q.astype(BF16)
        k_scr[:, qk_cols] = k * (dk ** -0.5)

        consts = []
        for direction in range(2):
            lg = _log_sigmoid(dec_ref[direction, hh])
            if direction == 0:
                dmat = jnp.where(rel >= 0, jnp.exp(jnp.maximum(rel, 0.0) * lg), 0.0)
                zeta = jnp.exp((chunk - 1 - ri) * lg)[:, :1]
                xi = jnp.exp((ri + 1) * lg)[:, :1]
            else:
                dmat = jnp.where(rel <= 0, jnp.exp(jnp.maximum(-rel, 0.0) * lg), 0.0)
                zeta = jnp.exp(ri * lg)[:, :1]
                xi = jnp.exp((chunk - ri) * lg)[:, :1]
            consts.append((dmat, zeta, xi, jnp.exp(chunk * lg)[:, :1]))
            if has_state:
                r_scr[direction, hh] = s0_ref[direction, hh].astype(F32)
            else:
                r_scr[direction, hh] = jnp.zeros((dk, dv), F32)

        def chunk_step(direction, i):
            dmat, zeta, xi, g_chunk = consts[direction]
            rows = slice(i * chunk, (i + 1) * chunk)
            qi = q_scr[rows, qk_cols]
            kf = k_scr[rows, qk_cols]
            vi = v_ref[rows, v_cols]
            s = lax.dot_general(qi, kf.astype(BF16), (((1,), (1,)), ((), ())),
                                preferred_element_type=F32)
            o_inner = jnp.dot((s * dmat).astype(BF16), vi, preferred_element_type=F32)
            o_cross = jnp.dot(qi, r_scr[direction, hh].astype(BF16),
                              preferred_element_type=F32) * xi
            kv = lax.dot_general((kf * zeta).astype(BF16), vi, (((0,), (0,)), ((), ())),
                                 preferred_element_type=F32)
            r_scr[direction, hh] = g_chunk * r_scr[direction, hh] + kv
            acc_scr[direction, rows, v_cols] = o_inner + o_cross

        for step in range(n_chunks):
            chunk_step(0, step)
            chunk_step(1, n_chunks - 1 - step)

        gn = gn_ref[:, v_cols]
        for i in range(n_chunks):
            rows = slice(i * chunk, (i + 1) * chunk)
            o = acc_scr[0, rows, v_cols] + acc_scr[1, rows, v_cols]
            mu = jnp.mean(o, axis=-1, keepdims=True)
            var = jnp.mean(jnp.square(o - mu), axis=-1, keepdims=True)
            on = ((o - mu) * lax.rsqrt(var + EPS)) * gn
            o_ref[rows, v_cols] = (jax.nn.silu(g_ref[rows, v_cols].astype(F32)) * on
                                   ).astype(o_ref.dtype)

    for hh in range(hb):
        head(hh)
    if emit_state:
        st_ref[...] = r_scr[...]


def _retention_core(proj, decay, gn_g, j, *, row0, n_seq, seq_len, state, rope, emit_state,
                    hb=2):
    heads = R_HEADS
    dk = proj.shape[1] // (6 * heads)
    dv = 2 * dk
    blk0 = row0 // seq_len
    hg = heads // hb
    dec = jnp.broadcast_to(decay[j][:, :, None, None], (2, heads, 1, LANES))
    in_specs = [pl.BlockSpec((seq_len, hb * dk), lambda b, h: (blk0 + b, h)),
                pl.BlockSpec((seq_len, hb * dk), lambda b, h: (blk0 + b, hg + h)),
                pl.BlockSpec((seq_len, hb * dv), lambda b, h: (blk0 + b, hg + h)),
                pl.BlockSpec((seq_len, hb * dv), lambda b, h: (blk0 + b, 2 * hg + h)),
                pl.BlockSpec((2, hb, 1, LANES), lambda b, h: (0, h, 0, 0)),
                _lead((j,), (1, hb * dv), lambda b, h: (0, h))]
    args = [proj, proj, proj, proj, dec, gn_g[:, None, :]]
    if rope is not None:
        in_specs += [pl.BlockSpec((seq_len, dk), lambda b, h: (0, 0))] * 2
        args += list(rope)
    if state is not None:
        in_specs.append(pl.BlockSpec((None, None, 2, hb, dk, dv),
                                     lambda b, h: (b, j, 0, h, 0, 0)))
        args.append(state)
    out_specs = [pl.BlockSpec((seq_len, hb * dv), lambda b, h: (b, h))]
    out_shape = [jax.ShapeDtypeStruct((n_seq * seq_len, heads * dv), BF16)]
    if emit_state:
        out_specs.append(pl.BlockSpec((None, 2, hb, dk, dv), lambda b, h: (b, 0, h, 0, 0)))
        out_shape.append(jax.ShapeDtypeStruct((n_seq, 2, heads, dk, dv), F32))
    body = functools.partial(_ret_kernel, seq_len=seq_len, dk=dk, hb=hb,
                             has_state=state is not None, has_rope=rope is not None,
                             emit_state=emit_state)
    return pl.pallas_call(
        body,
        grid=(n_seq, hg),
        in_specs=in_specs,
        out_specs=out_specs,
        out_shape=out_shape,
        scratch_shapes=[pltpu.VMEM((seq_len, hb * dk), BF16),
                        pltpu.VMEM((seq_len, hb * dk), F32),
                        pltpu.VMEM((2, seq_len, hb * dv), F32),
                        pltpu.VMEM((2, hb, dk, dv), F32)],
        compiler_params=_params(2),
        name="retention_core",
    )(*args)


def _rope_tables(seq_len, dk):
    nf = dk // 4
    inv = ROPE_BASE ** (-jnp.arange(nf, dtype=F32) / nf)
    t = jnp.arange(seq_len)
    tabs = []
    for p in (t // GRID_W, t % GRID_W):
        ang = p.astype(F32)[:, None] * inv
        tabs.append((jnp.cos(ang), jnp.sin(ang)))
    cos = jnp.concatenate([tabs[0][0], tabs[0][0], tabs[1][0], tabs[1][0]], axis=1)
    sin = jnp.concatenate([-tabs[0][1], tabs[0][1], -tabs[1][1], tabs[1][1]], axis=1)
    return cos, sin


def _lru_kernel(*refs, seq_len, has_state):
    it = iter(refs)
    y_ref, x_ref, cw_ref, cb_ref, gw_ref, gb_ref, lam_ref = (next(it) for _ in range(7))
    h0_ref = next(it) if has_state else None
    o_ref, st_ref, ab_scr, run_scr, h_scr = (next(it) for _ in range(5))
    n = seq_len
    x = x_ref[...].astype(F32)
    row = lax.broadcasted_iota(jnp.int32, x.shape, 0)

    def shifted(v, d, fill):
        if d == 0:
            return v
        if d > 0:
            return jnp.where(row >= d, pltpu.roll(v, d, 0), fill)
        return jnp.where(row < n + d, pltpu.roll(v, n + d, 0), fill)

    conv_w = cw_ref.shape[0]
    pad_l = conv_w // 2
    xc = shifted(x, pad_l, 0.0) * cw_ref[0:1, :]
    for t in range(1, conv_w):
        xc = xc + shifted(x, pad_l - t, 0.0) * cw_ref[t:t + 1, :]
    xc = xc + cb_ref[...]
    xcb = xc.astype(BF16)

    seg = n // SUBLANES
    pitch = seg + SEG_PAD
    n_slabs = x.shape[1] // LANES
    for direction in range(2):
        gates = [jax.nn.sigmoid(
            jnp.dot(xcb, gw_ref[direction, gi].astype(BF16), preferred_element_type=F32)
            + gb_ref[direction, gi:gi + 1, :]) for gi in range(2)]
        log_a = (LRU_C * gates[0]) * _log_sigmoid(lam_ref[direction:direction + 1, :])
        a = jnp.exp(log_a)
        b = (jnp.sqrt(-jnp.tanh(log_a) * (a * a + 1.0)) * gates[1]) * xc
        for sl in range(n_slabs):
            cols = slice(sl * LANES, (sl + 1) * LANES)
            for s in range(SUBLANES):
                src, dst = slice(s * seg, (s + 1) * seg), slice(s * pitch, s * pitch + seg)
                ab_scr[direction, 0, sl, dst, :] = a[src, cols]
                ab_scr[direction, 1, sl, dst, :] = b[src, cols]

    def step_rows(i):
        return pl.ds(i, SUBLANES, stride=pitch)

    chains = [(d, sl) for d in range(2) for sl in range(n_slabs)]

    def scan_body(i, carry):
        out = []
        for (d, sl), (a_run, h_run) in zip(chains, carry):
            rows = step_rows(i if d == 0 else seg - 1 - i)
            a_i = ab_scr[d, 0, sl, rows, :]
            h_run = a_i * h_run + ab_scr[d, 1, sl, rows, :]
            a_run = a_i * a_run
            run_scr[d, 0, sl, rows, :] = a_run
            run_scr[d, 1, sl, rows, :] = h_run
            out.append((a_run, h_run))
        return tuple(out)

    init = tuple((jnp.ones((SUBLANES, LANES), F32), jnp.zeros((SUBLANES, LANES), F32))
                 for _ in chains)
    ends = lax.fori_loop(0, seg, scan_body, init, unroll=SCAN_UNROLL)

    entering = []
    for (d, sl), (a_end, h_end) in zip(chains, ends):
        cols = slice(sl * LANES, (sl + 1) * LANES)
        c = h0_ref[d:d + 1, cols] if has_state else jnp.zeros((1, LANES), F32)
        rows = [None] * SUBLANES
        for s in (range(SUBLANES) if d == 0 else range(SUBLANES - 1, -1, -1)):
            rows[s] = c
            c = h_end[s:s + 1, :] + a_end[s:s + 1, :] * c
        st_ref[d:d + 1, cols] = c
        entering.append(jnp.concatenate(rows, axis=0))

    def fix_body(i, carry):
        rows = step_rows(i)
        for sl in range(n_slabs):
            h_f, h_b = (run_scr[d, 1, sl, rows, :]
                        + run_scr[d, 0, sl, rows, :] * entering[chains.index((d, sl))]
                        for d in range(2))
            h_scr[sl, rows, :] = h_f + h_b
        return carry

    lax.fori_loop(0, seg, fix_body, 0, unroll=SCAN_UNROLL)

    for sl in range(n_slabs):
        cols = slice(sl * LANES, (sl + 1) * LANES)
        for s in range(SUBLANES):
            src, dst = slice(s * pitch, s * pitch + seg), slice(s * seg, (s + 1) * seg)
            o_ref[dst, cols] = (h_scr[sl, src, :]
                                * jax.nn.gelu(y_ref[dst, cols].astype(F32))).astype(o_ref.dtype)


def _lru_core(proj, conv_w, conv_b, gate_w, gate_b, lam, j, *, row0, n_seq, seq_len, state):
    width = proj.shape[1] // 2
    nb = width // LRU_BLOCK
    blk0 = row0 // seq_len
    in_specs = [pl.BlockSpec((seq_len, LRU_BLOCK), lambda b, c: (blk0 + b, c)),
                pl.BlockSpec((seq_len, LRU_BLOCK), lambda b, c: (blk0 + b, nb + c)),
                _lead((j,), (conv_w.shape[1], LRU_BLOCK), lambda b, c: (0, c)),
                _lead((j,), (1, LRU_BLOCK), lambda b, c: (0, c)),
                _lead((j,), (2, 2, None, LRU_BLOCK, LRU_BLOCK), lambda b, c: (0, 0, c, 0, 0)),
                _lead((j,), (2, 2, LRU_BLOCK), lambda b, c: (0, 0, c)),
                _lead((j,), (2, LRU_BLOCK), lambda b, c: (0, c))]
    args = [proj, proj, conv_w, conv_b[:, None, :], gate_w, gate_b, lam]
    if state is not None:
        in_specs.append(pl.BlockSpec((None, None, 2, LRU_BLOCK), lambda b, c: (b, j, 0, c)))
        args.append(state)
    body = functools.partial(_lru_kernel, seq_len=seq_len, has_state=state is not None)
    n_slabs = LRU_BLOCK // LANES
    scr_rows = SUBLANES * (seq_len // SUBLANES + SEG_PAD)
    return pl.pallas_call(
        body,
        grid=(n_seq, nb),
        in_specs=in_specs,
        out_specs=[pl.BlockSpec((seq_len, LRU_BLOCK), lambda b, c: (b, c)),
                   pl.BlockSpec((None, 2, LRU_BLOCK), lambda b, c: (b, 0, c))],
        out_shape=[jax.ShapeDtypeStruct((n_seq * seq_len, width), BF16),
                   jax.ShapeDtypeStruct((n_seq, 2, width), F32)],
        scratch_shapes=[pltpu.VMEM((2, 2, n_slabs, scr_rows, LANES), F32),
                        pltpu.VMEM((2, 2, n_slabs, scr_rows, LANES), F32),
                        pltpu.VMEM((n_slabs, scr_rows, LANES), F32)],
        compiler_params=_params(2),
        name="lru_core",
    )(*args)


def _norm_kernel(x_ref, g_ref, o_ref):
    x = x_ref[...]
    o_ref[...] = (x * lax.rsqrt(jnp.mean(x * x, axis=-1, keepdims=True) + EPS)) * g_ref[...]


def _final_norm(x, gain, row0, rows, tr=512):
    d = x.shape[1]
    blk0 = row0 // tr
    return pl.pallas_call(
        _norm_kernel,
        grid=(rows // tr,),
        in_specs=[pl.BlockSpec((tr, d), lambda i: (blk0 + i, 0)),
                  pl.BlockSpec((1, d), lambda i: (0, 0))],
        out_specs=pl.BlockSpec((tr, d), lambda i: (i, 0)),
        out_shape=jax.ShapeDtypeStruct((rows, d), F32),
        compiler_params=_params(1),
        name="final_norm",
    )(x, gain.reshape(1, d))


def kernel(x_prompt, x_sample, state_ret, state_lru, c, c_ctx, norm_g, final_g, w_mod, b_mod,
           ffn_w_in, ffn_w_out, a_w_in, a_v_g, a_w_s, a_b_s, a_w_out, r_w_in, r_decay, r_gn_g,
           r_w_out, l_w_in, l_conv_w, l_conv_b, l_gate_w, l_gate_b, l_lambda, l_w_out):
    batch, seq, d = x_prompt.shape
    dec_batch, dec_seq, _ = x_sample.shape
    depth = w_mod.shape[0]
    m_prompt, m_sample = batch * seq, dec_batch * dec_seq
    assert m_prompt % TM == 0 and dec_seq % TM == 0 and dec_batch + 1 <= MOD_ROWS
    prompt_tiles, tiles_per_sample = m_prompt // TM, dec_seq // TM

    def seg(i):
        return jnp.maximum(i - prompt_tiles, -1) // tiles_per_sample + 1

    x = jnp.concatenate([x_prompt.reshape(m_prompt, d), x_sample.reshape(m_sample, d)], axis=0)
    cond = jnp.concatenate(
        [c_ctx[None, :], c, jnp.zeros((MOD_ROWS - 1 - dec_batch, d), F32)], axis=0)
    mod_all = _modulation(cond, w_mod, b_mod).reshape(depth, MOD_ROWS, N_MOD, d)
    rope = _rope_tables(dec_seq, d // R_HEADS)

    def ffn(x, l, s, mod, tn=512):
        hid, w_out = _project(x, norm_g[l, 2 * s], mod, ffn_w_in, (l, s), seg, shift_idx=6 * s,
                              scale_idx=6 * s + 1, act="swiglu", tn=tn,
                              cast=(ffn_w_out, (l, s)))
        return _out_project(hid, w_out, x, mod, seg, gate_idx=6 * s + 2, coef=0.5)

    ret_states, lru_states = [], []
    for l in range(depth):
        kind, j = l % N_MIXERS, l // N_MIXERS
        mod = mod_all[l]
        x = ffn(x, l, 0, mod)
        pre = dict(shift_idx=3, scale_idx=4, tn=1024)
        if kind == 0:
            uv, w_out = _project(x, norm_g[l, 1], mod, a_w_in, (j,), seg, act="gelu",
                                 cast=(a_w_out, (j,)), **pre)
            mixed = _gmlp_core(uv, a_v_g, a_w_s, a_b_s, j)
        elif kind == 1:
            proj, w_out = _project(x, norm_g[l, 1], mod, r_w_in, (j,), seg, act="none",
                                   cast=(r_w_out, (j,)), **pre)
            o_p, st = _retention_core(proj, r_decay, r_gn_g, j, row0=0, n_seq=batch,
                                      seq_len=seq, state=None, rope=None, emit_state=True,
                                      hb=4)
            (o_s,) = _retention_core(proj, r_decay, r_gn_g, j, row0=m_prompt, n_seq=dec_batch,
                                     seq_len=dec_seq, state=state_ret, rope=rope,
                                     emit_state=False)
            mixed = (o_p, o_s)
            ret_states.append(st)
        else:
            proj, w_out = _project(x, norm_g[l, 1], mod, l_w_in, (j,), seg, act="none",
                                   cast=(l_w_out, (j,)), **pre)
            o_p, st = _lru_core(proj, l_conv_w, l_conv_b, l_gate_w, l_gate_b, l_lambda, j, row0=0,
                                n_seq=batch, seq_len=seq, state=None)
            o_s, _ = _lru_core(proj, l_conv_w, l_conv_b, l_gate_w, l_gate_b, l_lambda, j,
                               row0=m_prompt, n_seq=dec_batch, seq_len=dec_seq, state=state_lru)
            mixed = (o_p, o_s)
            lru_states.append(st)
        x = _out_project(mixed, w_out, x, mod, seg, gate_idx=5, coef=1.0)
        x = ffn(x, l, 1, mod, tn=256 if l == depth - 1 else 512)

    y_prompt = _final_norm(x, final_g, 0, m_prompt).reshape(batch, seq, d)
    y_sample = _final_norm(x, final_g, m_prompt, m_sample).reshape(dec_batch, dec_seq, d)
    new_state_ret = jnp.stack(ret_states, axis=1)
    new_state_lru = jnp.stack(lru_states, axis=1)
    return (y_prompt, y_sample, new_state_ret, new_state_lru)
```

```python
import functools

import jax
import jax.numpy as jnp
from jax import lax
from jax.experimental import pallas as pl
from jax.experimental.pallas import tpu as pltpu

F32 = jnp.float32
BF16 = jnp.bfloat16

EPS = 1e-6
N_MIXERS = 3
N_MOD = 9
GRID_W = 64
A_CHUNK = 128
A_GROUPS = 8
R_HEADS = 8
R_CHUNK = 128
ROPE_BASE = 10000.0
LRU_BLOCK = 256
LRU_C = 8.0

LANES = 128
SUBLANES = 8
MXU_N = 256
SEG_PAD = 4
SCAN_UNROLL = 8
MOD_ROWS = SUBLANES
TM = 1024
ROW_CHUNK = 16
CAST_ROWS = 256
VMEM_LIMIT = 56 * 1024 * 1024


def _params(n_grid):
    return pltpu.CompilerParams(dimension_semantics=("arbitrary",) * n_grid,
                                vmem_limit_bytes=VMEM_LIMIT)


def _lead(lead, block, index_map):
    return pl.BlockSpec((None,) * len(lead) + tuple(block),
                        lambda *g: tuple(lead) + tuple(index_map(*g)))


def _log_sigmoid(x):
    return jnp.minimum(x, 0.0) - jnp.log1p(jnp.exp(-jnp.abs(x)))


def _mod_kernel(c_ref, w_ref, b_ref, o_ref):
    s = jax.nn.silu(c_ref[...]).astype(BF16)
    o_ref[...] = jnp.dot(s, w_ref[...].astype(BF16), preferred_element_type=F32) + b_ref[...]


def _modulation(cond, w_mod, b_mod, tn=2048):
    depth, d, n = w_mod.shape
    return pl.pallas_call(
        _mod_kernel,
        grid=(depth, n // tn),
        in_specs=[pl.BlockSpec((MOD_ROWS, d), lambda l, j: (0, 0)),
                  pl.BlockSpec((None, d, tn), lambda l, j: (l, 0, j)),
                  pl.BlockSpec((None, 1, tn), lambda l, j: (l, 0, j))],
        out_specs=pl.BlockSpec((None, MOD_ROWS, tn), lambda l, j: (l, 0, j)),
        out_shape=jax.ShapeDtypeStruct((depth, MOD_ROWS, n), F32),
        compiler_params=_params(2),
        name="modulation",
    )(cond, w_mod, b_mod.reshape(depth, 1, n))


def _proj_kernel(x_ref, g_ref, mod_ref, *refs, n_w, shift_idx, scale_idx, act, n_cast):
    w_refs = refs[:n_w]
    if n_cast:
        wc_ref, o_ref, wb_ref, h_ref = refs[n_w:]
    else:
        o_ref, h_ref = refs[n_w:]

    @pl.when(pl.program_id(1) == 0)
    def _prologue():
        gain = g_ref[...]
        scale = 1.0 + mod_ref[scale_idx:scale_idx + 1, :]
        shift = mod_ref[shift_idx:shift_idx + 1, :]

        def rows_body(r, carry):
            rows = pl.ds(pl.multiple_of(r * ROW_CHUNK, ROW_CHUNK), ROW_CHUNK)
            x = x_ref[rows, :]
            y = x * lax.rsqrt(jnp.mean(x * x, axis=-1, keepdims=True) + EPS)
            h_ref[rows, :] = ((y * gain) * scale + shift).astype(h_ref.dtype)
            return carry

        lax.fori_loop(0, x_ref.shape[0] // ROW_CHUNK, rows_body, 0, unroll=8)

    if n_cast:
        step = pl.program_id(0) * pl.num_programs(1) + pl.program_id(1)

        @pl.when(step < n_cast)
        def _cast():
            wb_ref[...] = wc_ref[...].astype(BF16)

    h = h_ref[...]
    for c in range(0, o_ref.shape[1], MXU_N):
        cols = slice(c, c + MXU_N)
        ys = [jnp.dot(h, w[:, cols].astype(BF16), preferred_element_type=F32) for w in w_refs]
        if act == "swiglu":
            out = jax.nn.silu(ys[0]) * ys[1]
        elif act == "gelu":
            out = jax.nn.gelu(ys[0])
        else:
            out = ys[0]
        o_ref[:, cols] = out.astype(o_ref.dtype)


def _project(x, gain, mod, w, lead, seg, *, shift_idx, scale_idx, act, tn, cast=None):
    m, d = x.shape
    n_total = w.shape[-1]
    if act == "swiglu":
        n_out = n_total // 2
        half = n_out // tn
        w_specs = [_lead(lead, (d, tn), lambda i, j: (0, j)),
                   _lead(lead, (d, tn), lambda i, j: (0, j + half))]
    else:
        n_out = n_total
        w_specs = [_lead(lead, (d, tn), lambda i, j: (0, j))]
    grid = (m // TM, n_out // tn)
    in_specs = [pl.BlockSpec((TM, d), lambda i, j: (i, 0)),
                pl.BlockSpec((1, d), lambda i, j: (0, 0)),
                pl.BlockSpec((None, N_MOD, d), lambda i, j: (seg(i), 0, 0))] + w_specs
    args = [x, gain.reshape(1, d), mod] + [w] * len(w_specs)
    out_specs = [pl.BlockSpec((TM, tn), lambda i, j: (i, j))]
    out_shape = [jax.ShapeDtypeStruct((m, n_out), BF16)]
    n_cast = 0
    if cast is not None:
        w_next, lead_next = cast
        k_next, d_next = w_next.shape[-2:]
        n_cast = k_next // CAST_ROWS
        assert k_next % CAST_ROWS == 0 and n_cast <= grid[0] * grid[1]

        def chunk(i, j):
            return (jnp.minimum(i * grid[1] + j, n_cast - 1), 0)

        in_specs.append(_lead(lead_next, (CAST_ROWS, d_next), chunk))
        args.append(w_next)
        out_specs.append(pl.BlockSpec((CAST_ROWS, d_next), chunk))
        out_shape.append(jax.ShapeDtypeStruct((k_next, d_next), BF16))
    body = functools.partial(_proj_kernel, n_w=len(w_specs), shift_idx=shift_idx,
                             scale_idx=scale_idx, act=act, n_cast=n_cast)
    outs = pl.pallas_call(
        body,
        grid=grid,
        in_specs=in_specs,
        out_specs=out_specs,
        out_shape=out_shape,
        scratch_shapes=[pltpu.VMEM((TM, d), BF16)],
        compiler_params=_params(2),
        name="project_" + act,
    )(*args)
    return outs if cast is not None else outs[0]


def _out_kernel(*refs, n_a, split, gate_idx, coef):
    a_refs = refs[:n_a]
    w_ref, x_ref, mod_ref, o_ref = refs[n_a:]

    def update(a_ref):
        y = jnp.dot(a_ref[...], w_ref[...], preferred_element_type=F32)
        gate = mod_ref[gate_idx:gate_idx + 1, :]
        if coef != 1.0:
            gate = coef * gate
        o_ref[...] = x_ref[...] + gate * y

    if n_a == 1:
        update(a_refs[0])
    else:
        pl.when(pl.program_id(0) < split)(functools.partial(update, a_refs[0]))
        pl.when(pl.program_id(0) >= split)(functools.partial(update, a_refs[1]))


def _out_project(a, w, x, mod, seg, *, gate_idx, coef, tn=512):
    m, d = x.shape
    assert w.dtype == BF16
    if isinstance(a, tuple):
        split = a[0].shape[0] // TM
        k = a[0].shape[1]
        a_specs = [pl.BlockSpec((TM, k), lambda i, j: (jnp.minimum(i, split - 1), 0)),
                   pl.BlockSpec((TM, k), lambda i, j: (jnp.maximum(i - split, 0), 0))]
    else:
        split, k = 0, a.shape[1]
        a_specs = [pl.BlockSpec((TM, k), lambda i, j: (i, 0))]
        a = (a,)
    body = functools.partial(_out_kernel, n_a=len(a), split=split, gate_idx=gate_idx, coef=coef)
    return pl.pallas_call(
        body,
        grid=(m // TM, d // tn),
        in_specs=a_specs + [pl.BlockSpec((k, tn), lambda i, j: (0, j)),
                            pl.BlockSpec((TM, tn), lambda i, j: (i, j)),
                            pl.BlockSpec((None, N_MOD, tn), lambda i, j: (seg(i), 0, j))],
        out_specs=pl.BlockSpec((TM, tn), lambda i, j: (i, j)),
        out_shape=jax.ShapeDtypeStruct((m, d), F32),
        compiler_params=_params(2),
        name="out_project",
    )(*a, w, x, mod)


def _gmlp_kernel(u_ref, v_ref, vg_ref, ws_ref, bs_ref, o_ref):
    width = v_ref.shape[1]
    gw = width // A_GROUPS
    vg = vg_ref[...]
    for c in range(v_ref.shape[0] // A_CHUNK):
        rows = slice(c * A_CHUNK, (c + 1) * A_CHUNK)
        v = v_ref[rows, :].astype(F32)
        vn = v * lax.rsqrt(jnp.mean(v * v, axis=-1, keepdims=True) + EPS)
        vn = (vn * vg).astype(BF16)
        for g in range(A_GROUPS):
            cols = slice(g * gw, (g + 1) * gw)
            sv = jnp.dot(ws_ref[g].astype(BF16), vn[:, cols], preferred_element_type=F32)
            sv = sv + bs_ref[:, g:g + 1]
            o_ref[rows, cols] = (u_ref[rows, cols].astype(F32) * sv).astype(o_ref.dtype)


def _gmlp_core(uv, v_g, w_s, b_s, j, rows_per_step=4 * A_CHUNK):
    m, two_w = uv.shape
    width = two_w // 2
    tr = rows_per_step
    return pl.pallas_call(
        _gmlp_kernel,
        grid=(m // tr,),
        in_specs=[pl.BlockSpec((tr, width), lambda i: (i, 0)),
                  pl.BlockSpec((tr, width), lambda i: (i, 1)),
                  _lead((j,), (1, width), lambda i: (0, 0)),
                  _lead((j,), (A_GROUPS, A_CHUNK, A_CHUNK), lambda i: (0, 0, 0)),
                  _lead((j,), (A_CHUNK, A_GROUPS), lambda i: (0, 0))],
        out_specs=pl.BlockSpec((tr, width), lambda i: (i, 0)),
        out_shape=jax.ShapeDtypeStruct((m, width), BF16),
        compiler_params=_params(1),
        name="gmlp_core",
    )(uv, uv, v_g[:, None, :], w_s, jnp.swapaxes(b_s, 1, 2))


def _ret_kernel(*refs, seq_len, dk, hb, has_state, has_rope, emit_state):
    it = iter(refs)
    q_ref, k_ref, v_ref, g_ref, dec_ref, gn_ref = (next(it) for _ in range(6))
    cos_ref, sin_ref = (next(it), next(it)) if has_rope else (None, None)
    s0_ref = next(it) if has_state else None
    o_ref = next(it)
    st_ref = next(it) if emit_state else None
    q_scr, k_scr, acc_scr, r_scr = (next(it) for _ in range(4))
    chunk = R_CHUNK
    n_chunks = seq_len // chunk
    dv = 2 * dk

    ri = lax.broadcasted_iota(jnp.int32, (chunk, chunk), 0).astype(F32)
    ci = lax.broadcasted_iota(jnp.int32, (chunk, chunk), 1).astype(F32)
    rel = ri - ci

    def head(hh):
        qk_cols = slice(hh * dk, (hh + 1) * dk)
        v_cols = slice(hh * dv, (hh + 1) * dv)
        q = q_ref[:, qk_cols].astype(F32)
        k = k_ref[:, qk_cols].astype(F32)
        if has_rope:
            cos, sin = cos_ref[...], sin_ref[...]

            def rope(t):
                swapped = jnp.concatenate(
                    [pltpu.roll(t[:, s:s + LANES], LANES // 2, 1) for s in range(0, dk, LANES)],
                    axis=1)
                return t * cos + swapped * sin

            q, k = rope(q), rope(k)
        q_scr[:, qk_cols] = q.astype(BF16)
        k_scr[:, qk_cols] = k * (dk ** -0.5)

        consts = []
        for direction in range(2):
            lg = _log_sigmoid(dec_ref[direction, hh])
            if direction == 0:
                dmat = jnp.where(rel >= 0, jnp.exp(jnp.maximum(rel, 0.0) * lg), 0.0)
                zeta = jnp.exp((chunk - 1 - ri) * lg)[:, :1]
                xi = jnp.exp((ri + 1) * lg)[:, :1]
            else:
                dmat = jnp.where(rel <= 0, jnp.exp(jnp.maximum(-rel, 0.0) * lg), 0.0)
                zeta = jnp.exp(ri * lg)[:, :1]
                xi = jnp.exp((chunk - ri) * lg)[:, :1]
            consts.append((dmat, zeta, xi, jnp.exp(chunk * lg)[:, :1]))
            if has_state:
                r_scr[direction, hh] = s0_ref[direction, hh].astype(F32)
            else:
                r_scr[direction, hh] = jnp.zeros((dk, dv), F32)

        def chunk_step(direction, i):
            dmat, zeta, xi, g_chunk = consts[direction]
            rows = slice(i * chunk, (i + 1) * chunk)
            qi = q_scr[rows, qk_cols]
            kf = k_scr[rows, qk_cols]
            vi = v_ref[rows, v_cols]
            s = lax.dot_general(qi, kf.astype(BF16), (((1,), (1,)), ((), ())),
                                preferred_element_type=F32)
            o_inner = jnp.dot((s * dmat).astype(BF16), vi, preferred_element_type=F32)
            o_cross = jnp.dot(qi, r_scr[direction, hh].astype(BF16),
                              preferred_element_type=F32) * xi
            kv = lax.dot_general((kf * zeta).astype(BF16), vi, (((0,), (0,)), ((), ())),
                                 preferred_element_type=F32)
            r_scr[direction, hh] = g_chunk * r_scr[direction, hh] + kv
            acc_scr[direction, rows, v_cols] = o_inner + o_cross

        for step in range(n_chunks):
            chunk_step(0, step)
            chunk_step(1, n_chunks - 1 - step)

        gn = gn_ref[:, v_cols]
        for i in range(n_chunks):
            rows = slice(i * chunk, (i + 1) * chunk)
            o = acc_scr[0, rows, v_cols] + acc_scr[1, rows, v_cols]
            mu = jnp.mean(o, axis=-1, keepdims=True)
            var = jnp.mean(jnp.square(o - mu), axis=-1, keepdims=True)
            on = ((o - mu) * lax.rsqrt(var + EPS)) * gn
            o_ref[rows, v_cols] = (jax.nn.silu(g_ref[rows, v_cols].astype(F32)) * on
                                   ).astype(o_ref.dtype)

    for hh in range(hb):
        head(hh)
    if emit_state:
        st_ref[...] = r_scr[...]


def _retention_core(proj, decay, gn_g, j, *, row0, n_seq, seq_len, state, rope, emit_state,
                    hb=2):
    heads = R_HEADS
    dk = proj.shape[1] // (6 * heads)
    dv = 2 * dk
    blk0 = row0 // seq_len
    hg = heads // hb
    dec = jnp.broadcast_to(decay[j][:, :, None, None], (2, heads, 1, LANES))
    in_specs = [pl.BlockSpec((seq_len, hb * dk), lambda b, h: (blk0 + b, h)),
                pl.BlockSpec((seq_len, hb * dk), lambda b, h: (blk0 + b, hg + h)),
                pl.BlockSpec((seq_len, hb * dv), lambda b, h: (blk0 + b, hg + h)),
                pl.BlockSpec((seq_len, hb * dv), lambda b, h: (blk0 + b, 2 * hg + h)),
                pl.BlockSpec((2, hb, 1, LANES), lambda b, h: (0, h, 0, 0)),
                _lead((j,), (1, hb * dv), lambda b, h: (0, h))]
    args = [proj, proj, proj, proj, dec, gn_g[:, None, :]]
    if rope is not None:
        in_specs += [pl.BlockSpec((seq_len, dk), lambda b, h: (0, 0))] * 2
        args += list(rope)
    if state is not None:
        in_specs.append(pl.BlockSpec((None, None, 2, hb, dk, dv),
                                     lambda b, h: (b, j, 0, h, 0, 0)))
        args.append(state)
    out_specs = [pl.BlockSpec((seq_len, hb * dv), lambda b, h: (b, h))]
    out_shape = [jax.ShapeDtypeStruct((n_seq * seq_len, heads * dv), BF16)]
    if emit_state:
        out_specs.append(pl.BlockSpec((None, 2, hb, dk, dv), lambda b, h: (b, 0, h, 0, 0)))
        out_shape.append(jax.ShapeDtypeStruct((n_seq, 2, heads, dk, dv), F32))
    body = functools.partial(_ret_kernel, seq_len=seq_len, dk=dk, hb=hb,
                             has_state=state is not None, has_rope=rope is not None,
                             emit_state=emit_state)
    return pl.pallas_call(
        body,
        grid=(n_seq, hg),
        in_specs=in_specs,
        out_specs=out_specs,
        out_shape=out_shape,
        scratch_shapes=[pltpu.VMEM((seq_len, hb * dk), BF16),
                        pltpu.VMEM((seq_len, hb * dk), F32),
                        pltpu.VMEM((2, seq_len, hb * dv), F32),
                        pltpu.VMEM((2, hb, dk, dv), F32)],
        compiler_params=_params(2),
        name="retention_core",
    )(*args)


def _rope_tables(seq_len, dk):
    nf = dk // 4
    inv = ROPE_BASE ** (-jnp.arange(nf, dtype=F32) / nf)
    t = jnp.arange(seq_len)
    tabs = []
    for p in (t // GRID_W, t % GRID_W):
        ang = p.astype(F32)[:, None] * inv
        tabs.append((jnp.cos(ang), jnp.sin(ang)))
    cos = jnp.concatenate([tabs[0][0], tabs[0][0], tabs[1][0], tabs[1][0]], axis=1)
    sin = jnp.concatenate([-tabs[0][1], tabs[0][1], -tabs[1][1], tabs[1][1]], axis=1)
    return cos, sin


def _lru_kernel(*refs, seq_len, n_blk, has_state):
    it = iter(refs)
    y_ref, x_ref, cw_ref, cb_ref, gw_ref, gb_ref, lam_ref = (next(it) for _ in range(7))
    h0_ref = next(it) if has_state else None
    o_ref, st_ref, ab_scr, run_scr, h_scr = (next(it) for _ in range(5))
    n = seq_len
    row = lax.broadcasted_iota(jnp.int32, (n, LRU_BLOCK), 0)

    def shifted(v, d, fill):
        if d == 0:
            return v
        if d > 0:
            return jnp.where(row >= d, pltpu.roll(v, d, 0), fill)
        return jnp.where(row < n + d, pltpu.roll(v, n + d, 0), fill)

    seg = n // SUBLANES
    pitch = seg + SEG_PAD
    n_slabs = x_ref.shape[1] // LANES
    slabs_per_block = LRU_BLOCK // LANES
    conv_w = cw_ref.shape[0]
    pad_l = conv_w // 2
    for blk in range(n_blk):
        bc = slice(blk * LRU_BLOCK, (blk + 1) * LRU_BLOCK)
        x = x_ref[:, bc].astype(F32)
        xc = shifted(x, pad_l, 0.0) * cw_ref[0:1, bc]
        for t in range(1, conv_w):
            xc = xc + shifted(x, pad_l - t, 0.0) * cw_ref[t:t + 1, bc]
        xc = xc + cb_ref[:, bc]
        xcb = xc.astype(BF16)
        for direction in range(2):
            gates = [jax.nn.sigmoid(
                jnp.dot(xcb, gw_ref[direction, gi, blk].astype(BF16), preferred_element_type=F32)
                + gb_ref[direction, gi:gi + 1, bc]) for gi in range(2)]
            log_a = (LRU_C * gates[0]) * _log_sigmoid(lam_ref[direction:direction + 1, bc])
            a = jnp.exp(log_a)
            b = (jnp.sqrt(-jnp.tanh(log_a) * (a * a + 1.0)) * gates[1]) * xc
            for sl in range(slabs_per_block):
                slab = blk * slabs_per_block + sl
                cols = slice(sl * LANES, (sl + 1) * LANES)
                for s in range(SUBLANES):
                    src, dst = slice(s * seg, (s + 1) * seg), slice(s * pitch, s * pitch + seg)
                    ab_scr[direction, 0, slab, dst, :] = a[src, cols]
                    ab_scr[direction, 1, slab, dst, :] = b[src, cols]

    def step_rows(i):
        return pl.ds(i, SUBLANES, stride=pitch)

    chains = [(d, sl) for d in range(2) for sl in range(n_slabs)]

    def scan_body(i, carry):
        out = []
        for (d, sl), (a_run, h_run) in zip(chains, carry):
            rows = step_rows(i if d == 0 else seg - 1 - i)
            a_i = ab_scr[d, 0, sl, rows, :]
            h_run = a_i * h_run + ab_scr[d, 1, sl, rows, :]
            a_run = a_i * a_run
            run_scr[d, 0, sl, rows, :] = a_run
            run_scr[d, 1, sl, rows, :] = h_run
            out.append((a_run, h_run))
        return tuple(out)

    init = tuple((jnp.ones((SUBLANES, LANES), F32), jnp.zeros((SUBLANES, LANES), F32))
                 for _ in chains)
    ends = lax.fori_loop(0, seg, scan_body, init, unroll=SCAN_UNROLL)

    entering = []
    for (d, sl), (a_end, h_end) in zip(chains, ends):
        cols = slice(sl * LANES, (sl + 1) * LANES)
        c = h0_ref[d:d + 1, cols] if has_state else jnp.zeros((1, LANES), F32)
        rows = [None] * SUBLANES
        for s in (range(SUBLANES) if d == 0 else range(SUBLANES - 1, -1, -1)):
            rows[s] = c
            c = h_end[s:s + 1, :] + a_end[s:s + 1, :] * c
        st_ref[d:d + 1, cols] = c
        entering.append(jnp.concatenate(rows, axis=0))

    def fix_body(i, carry):
        rows = step_rows(i)
        for sl in range(n_slabs):
            h_f, h_b = (run_scr[d, 1, sl, rows, :]
                        + run_scr[d, 0, sl, rows, :] * entering[chains.index((d, sl))]
                        for d in range(2))
            h_scr[sl, rows, :] = h_f + h_b
        return carry

    lax.fori_loop(0, seg, fix_body, 0, unroll=SCAN_UNROLL)

    for sl in range(n_slabs):
        cols = slice(sl * LANES, (sl + 1) * LANES)
        for s in range(SUBLANES):
            src, dst = slice(s * pitch, s * pitch + seg), slice(s * seg, (s + 1) * seg)
            o_ref[dst, cols] = (h_scr[sl, src, :]
                                * jax.nn.gelu(y_ref[dst, cols].astype(F32))).astype(o_ref.dtype)


def _lru_core(proj, conv_w, conv_b, gate_w, gate_b, lam, j, *, row0, n_seq, seq_len, state,
              n_blk=2):
    width = proj.shape[1] // 2
    tw = n_blk * LRU_BLOCK
    nb = width // tw
    blk0 = row0 // seq_len
    in_specs = [pl.BlockSpec((seq_len, tw), lambda b, c: (blk0 + b, c)),
                pl.BlockSpec((seq_len, tw), lambda b, c: (blk0 + b, nb + c)),
                _lead((j,), (conv_w.shape[1], tw), lambda b, c: (0, c)),
                _lead((j,), (1, tw), lambda b, c: (0, c)),
                _lead((j,), (2, 2, n_blk, LRU_BLOCK, LRU_BLOCK), lambda b, c: (0, 0, c, 0, 0)),
                _lead((j,), (2, 2, tw), lambda b, c: (0, 0, c)),
                _lead((j,), (2, tw), lambda b, c: (0, c))]
    args = [proj, proj, conv_w, conv_b[:, None, :], gate_w, gate_b, lam]
    if state is not None:
        in_specs.append(pl.BlockSpec((None, None, 2, tw), lambda b, c: (b, j, 0, c)))
        args.append(state)
    body = functools.partial(_lru_kernel, seq_len=seq_len, n_blk=n_blk,
                             has_state=state is not None)
    n_slabs = tw // LANES
    scr_rows = SUBLANES * (seq_len // SUBLANES + SEG_PAD)
    return pl.pallas_call(
        body,
        grid=(n_seq, nb),
        in_specs=in_specs,
        out_specs=[pl.BlockSpec((seq_len, tw), lambda b, c: (b, c)),
                   pl.BlockSpec((None, 2, tw), lambda b, c: (b, 0, c))],
        out_shape=[jax.ShapeDtypeStruct((n_seq * seq_len, width), BF16),
                   jax.ShapeDtypeStruct((n_seq, 2, width), F32)],
        scratch_shapes=[pltpu.VMEM((2, 2, n_slabs, scr_rows, LANES), F32),
                        pltpu.VMEM((2, 2, n_slabs, scr_rows, LANES), F32),
                        pltpu.VMEM((n_slabs, scr_rows, LANES), F32)],
        compiler_params=_params(2),
        name="lru_core",
    )(*args)


def _norm_kernel(x_ref, g_ref, o_ref):
    x = x_ref[...]
    o_ref[...] = (x * lax.rsqrt(jnp.mean(x * x, axis=-1, keepdims=True) + EPS)) * g_ref[...]


def _final_norm(x, gain, row0, rows, tr=512):
    d = x.shape[1]
    blk0 = row0 // tr
    return pl.pallas_call(
        _norm_kernel,
        grid=(rows // tr,),
        in_specs=[pl.BlockSpec((tr, d), lambda i: (blk0 + i, 0)),
                  pl.BlockSpec((1, d), lambda i: (0, 0))],
        out_specs=pl.BlockSpec((tr, d), lambda i: (i, 0)),
        out_shape=jax.ShapeDtypeStruct((rows, d), F32),
        compiler_params=_params(1),
        name="final_norm",
    )(x, gain.reshape(1, d))


def kernel(x_prompt, x_sample, state_ret, state_lru, c, c_ctx, norm_g, final_g, w_mod, b_mod,
           ffn_w_in, ffn_w_out, a_w_in, a_v_g, a_w_s, a_b_s, a_w_out, r_w_in, r_decay, r_gn_g,
           r_w_out, l_w_in, l_conv_w, l_conv_b, l_gate_w, l_gate_b, l_lambda, l_w_out):
    batch, seq, d = x_prompt.shape
    dec_batch, dec_seq, _ = x_sample.shape
    depth = w_mod.shape[0]
    m_prompt, m_sample = batch * seq, dec_batch * dec_seq
    assert m_prompt % TM == 0 and dec_seq % TM == 0 and dec_batch + 1 <= MOD_ROWS
    prompt_tiles, tiles_per_sample = m_prompt // TM, dec_seq // TM

    def seg(i):
        return jnp.maximum(i - prompt_tiles, -1) // tiles_per_sample + 1

    x = jnp.concatenate([x_prompt.reshape(m_prompt, d), x_sample.reshape(m_sample, d)], axis=0)
    cond = jnp.concatenate(
        [c_ctx[None, :], c, jnp.zeros((MOD_ROWS - 1 - dec_batch, d), F32)], axis=0)
    mod_all = _modulation(cond, w_mod, b_mod).reshape(depth, MOD_ROWS, N_MOD, d)
    rope = _rope_tables(dec_seq, d // R_HEADS)

    def ffn(x, l, s, mod, tn=512):
        hid, w_out = _project(x, norm_g[l, 2 * s], mod, ffn_w_in, (l, s), seg, shift_idx=6 * s,
                              scale_idx=6 * s + 1, act="swiglu", tn=tn,
                              cast=(ffn_w_out, (l, s)))
        return _out_project(hid, w_out, x, mod, seg, gate_idx=6 * s + 2, coef=0.5)

    ret_states, lru_states = [], []
    for l in range(depth):
        kind, j = l % N_MIXERS, l // N_MIXERS
        mod = mod_all[l]
        x = ffn(x, l, 0, mod)
        pre = dict(shift_idx=3, scale_idx=4, tn=1024)
        if kind == 0:
            uv, w_out = _project(x, norm_g[l, 1], mod, a_w_in, (j,), seg, act="gelu",
                                 cast=(a_w_out, (j,)), **pre)
            mixed = _gmlp_core(uv, a_v_g, a_w_s, a_b_s, j)
        elif kind == 1:
            proj, w_out = _project(x, norm_g[l, 1], mod, r_w_in, (j,), seg, act="none",
                                   cast=(r_w_out, (j,)), **pre)
            o_p, st = _retention_core(proj, r_decay, r_gn_g, j, row0=0, n_seq=batch,
                                      seq_len=seq, state=None, rope=None, emit_state=True,
                                      hb=4)
            (o_s,) = _retention_core(proj, r_decay, r_gn_g, j, row0=m_prompt, n_seq=dec_batch,
                                     seq_len=dec_seq, state=state_ret, rope=rope,
                                     emit_state=False)
            mixed = (o_p, o_s)
            ret_states.append(st)
        else:
            proj, w_out = _project(x, norm_g[l, 1], mod, l_w_in, (j,), seg, act="none",
                                   cast=(l_w_out, (j,)), **pre)
            o_p, st = _lru_core(proj, l_conv_w, l_conv_b, l_gate_w, l_gate_b, l_lambda, j, row0=0,
                                n_seq=batch, seq_len=seq, state=None)
            o_s, _ = _lru_core(proj, l_conv_w, l_conv_b, l_gate_w, l_gate_b, l_lambda, j,
                               row0=m_prompt, n_seq=dec_batch, seq_len=dec_seq, state=state_lru)
            mixed = (o_p, o_s)
            lru_states.append(st)
        x = _out_project(mixed, w_out, x, mod, seg, gate_idx=5, coef=1.0)
        x = ffn(x, l, 1, mod)

    y_prompt = _final_norm(x, final_g, 0, m_prompt).reshape(batch, seq, d)
    y_sample = _final_norm(x, final_g, m_prompt, m_sample).reshape(dec_batch, dec_seq, d)
    new_state_ret = jnp.stack(ret_states, axis=1)
    new_state_lru = jnp.stack(lru_states, axis=1)
    return (y_prompt, y_sample, new_state_ret, new_state_lru)
```

```python
import functools

import jax
import jax.numpy as jnp
from jax import lax
from jax.experimental import pallas as pl
from jax.experimental.pallas import tpu as pltpu

F32 = jnp.float32
BF16 = jnp.bfloat16

EPS = 1e-6
N_MIXERS = 3
N_MOD = 9
GRID_W = 64
A_CHUNK = 128
A_GROUPS = 8
R_HEADS = 8
R_CHUNK = 256
ROPE_BASE = 10000.0
LRU_BLOCK = 256
LRU_C = 8.0

LANES = 128
SUBLANES = 8
MXU_N = 256
SEG_PAD = 4
SCAN_UNROLL = 8
MOD_ROWS = SUBLANES
TM = 1024
ROW_CHUNK = 16
CAST_ROWS = 256
VMEM_LIMIT = 56 * 1024 * 1024
OUT_VMEM_BUDGET = 52 * 1024 * 1024


def _params(n_grid):
    return pltpu.CompilerParams(dimension_semantics=("arbitrary",) * n_grid,
                                vmem_limit_bytes=VMEM_LIMIT)


def _lead(lead, block, index_map):
    return pl.BlockSpec((None,) * len(lead) + tuple(block),
                        lambda *g: tuple(lead) + tuple(index_map(*g)))


def _log_sigmoid(x):
    return jnp.minimum(x, 0.0) - jnp.log1p(jnp.exp(-jnp.abs(x)))


def _mod_kernel(c_ref, w_ref, b_ref, o_ref):
    s = jax.nn.silu(c_ref[...]).astype(BF16)
    o_ref[...] = jnp.dot(s, w_ref[...].astype(BF16), preferred_element_type=F32) + b_ref[...]


def _modulation(cond, w_mod, b_mod, tn=2048):
    depth, d, n = w_mod.shape
    return pl.pallas_call(
        _mod_kernel,
        grid=(depth, n // tn),
        in_specs=[pl.BlockSpec((MOD_ROWS, d), lambda l, j: (0, 0)),
                  pl.BlockSpec((None, d, tn), lambda l, j: (l, 0, j)),
                  pl.BlockSpec((None, 1, tn), lambda l, j: (l, 0, j))],
        out_specs=pl.BlockSpec((None, MOD_ROWS, tn), lambda l, j: (l, 0, j)),
        out_shape=jax.ShapeDtypeStruct((depth, MOD_ROWS, n), F32),
        compiler_params=_params(2),
        name="modulation",
    )(cond, w_mod, b_mod.reshape(depth, 1, n))


def _proj_kernel(x_ref, g_ref, mod_ref, *refs, n_w, shift_idx, scale_idx, act, n_cast):
    w_refs = refs[:n_w]
    if n_cast:
        wc_ref, o_ref, wb_ref, h_ref = refs[n_w:]
    else:
        o_ref, h_ref = refs[n_w:]

    @pl.when(pl.program_id(1) == 0)
    def _prologue():
        gain = g_ref[...]
        scale = 1.0 + mod_ref[scale_idx:scale_idx + 1, :]
        shift = mod_ref[shift_idx:shift_idx + 1, :]

        def rows_body(r, carry):
            rows = pl.ds(pl.multiple_of(r * ROW_CHUNK, ROW_CHUNK), ROW_CHUNK)
            x = x_ref[rows, :]
            y = x * lax.rsqrt(jnp.mean(x * x, axis=-1, keepdims=True) + EPS)
            h_ref[rows, :] = ((y * gain) * scale + shift).astype(h_ref.dtype)
            return carry

        lax.fori_loop(0, x_ref.shape[0] // ROW_CHUNK, rows_body, 0, unroll=8)

    if n_cast:
        step = pl.program_id(0) * pl.num_programs(1) + pl.program_id(1)

        @pl.when(step < n_cast)
        def _cast():
            wb_ref[...] = wc_ref[...].astype(BF16)

    h = h_ref[...]
    for c in range(0, o_ref.shape[1], MXU_N):
        cols = slice(c, c + MXU_N)
        ys = [jnp.dot(h, w[:, cols].astype(BF16), preferred_element_type=F32) for w in w_refs]
        if act == "swiglu":
            out = jax.nn.silu(ys[0]) * ys[1]
        elif act == "gelu":
            out = jax.nn.gelu(ys[0])
        else:
            out = ys[0]
        o_ref[:, cols] = out.astype(o_ref.dtype)


def _project(x, gain, mod, w, lead, seg, *, shift_idx, scale_idx, act, tn, cast=None):
    m, d = x.shape
    n_total = w.shape[-1]
    if act == "swiglu":
        n_out = n_total // 2
        half = n_out // tn
        w_specs = [_lead(lead, (d, tn), lambda i, j: (0, j)),
                   _lead(lead, (d, tn), lambda i, j: (0, j + half))]
    else:
        n_out = n_total
        w_specs = [_lead(lead, (d, tn), lambda i, j: (0, j))]
    grid = (m // TM, n_out // tn)
    in_specs = [pl.BlockSpec((TM, d), lambda i, j: (i, 0)),
                pl.BlockSpec((1, d), lambda i, j: (0, 0)),
                pl.BlockSpec((None, N_MOD, d), lambda i, j: (seg(i), 0, 0))] + w_specs
    args = [x, gain.reshape(1, d), mod] + [w] * len(w_specs)
    out_specs = [pl.BlockSpec((TM, tn), lambda i, j: (i, j))]
    out_shape = [jax.ShapeDtypeStruct((m, n_out), BF16)]
    n_cast = 0
    if cast is not None:
        w_next, lead_next = cast
        k_next, d_next = w_next.shape[-2:]
        n_cast = k_next // CAST_ROWS
        assert k_next % CAST_ROWS == 0 and n_cast <= grid[0] * grid[1]

        def chunk(i, j):
            return (jnp.minimum(i * grid[1] + j, n_cast - 1), 0)

        in_specs.append(_lead(lead_next, (CAST_ROWS, d_next), chunk))
        args.append(w_next)
        out_specs.append(pl.BlockSpec((CAST_ROWS, d_next), chunk))
        out_shape.append(jax.ShapeDtypeStruct((k_next, d_next), BF16))
    body = functools.partial(_proj_kernel, n_w=len(w_specs), shift_idx=shift_idx,
                             scale_idx=scale_idx, act=act, n_cast=n_cast)
    outs = pl.pallas_call(
        body,
        grid=grid,
        in_specs=in_specs,
        out_specs=out_specs,
        out_shape=out_shape,
        scratch_shapes=[pltpu.VMEM((TM, d), BF16)],
        compiler_params=_params(2),
        name="project_" + act,
    )(*args)
    return outs if cast is not None else outs[0]


def _out_kernel(*refs, n_a, split, gate_idx, coef):
    a_refs = refs[:n_a]
    w_ref, x_ref, mod_ref, o_ref = refs[n_a:]

    def update(a_ref):
        y = jnp.dot(a_ref[...], w_ref[...], preferred_element_type=F32)
        gate = mod_ref[gate_idx:gate_idx + 1, :]
        if coef != 1.0:
            gate = coef * gate
        o_ref[...] = x_ref[...] + gate * y

    if n_a == 1:
        update(a_refs[0])
    else:
        pl.when(pl.program_id(0) < split)(functools.partial(update, a_refs[0]))
        pl.when(pl.program_id(0) >= split)(functools.partial(update, a_refs[1]))


def _out_tile_n(n_a, k, d):
    for tn in (2048, 1024, 512, 256):
        blocks = 2 * (n_a * TM * k * 2 + k * tn * 2 + 2 * TM * tn * 4) + TM * tn * 4
        if d % tn == 0 and blocks <= OUT_VMEM_BUDGET:
            return tn
    raise ValueError("out_project tiles do not fit VMEM")


def _out_project(a, w, x, mod, seg, *, gate_idx, coef):
    m, d = x.shape
    assert w.dtype == BF16
    tn = _out_tile_n(len(a) if isinstance(a, tuple) else 1, w.shape[0], d)
    if isinstance(a, tuple):
        split = a[0].shape[0] // TM
        k = a[0].shape[1]
        a_specs = [pl.BlockSpec((TM, k), lambda i, j: (jnp.minimum(i, split - 1), 0)),
                   pl.BlockSpec((TM, k), lambda i, j: (jnp.maximum(i - split, 0), 0))]
    else:
        split, k = 0, a.shape[1]
        a_specs = [pl.BlockSpec((TM, k), lambda i, j: (i, 0))]
        a = (a,)
    body = functools.partial(_out_kernel, n_a=len(a), split=split, gate_idx=gate_idx, coef=coef)
    return pl.pallas_call(
        body,
        grid=(m // TM, d // tn),
        in_specs=a_specs + [pl.BlockSpec((k, tn), lambda i, j: (0, j)),
                            pl.BlockSpec((TM, tn), lambda i, j: (i, j)),
                            pl.BlockSpec((None, N_MOD, tn), lambda i, j: (seg(i), 0, j))],
        out_specs=pl.BlockSpec((TM, tn), lambda i, j: (i, j)),
        out_shape=jax.ShapeDtypeStruct((m, d), F32),
        compiler_params=_params(2),
        name="out_project",
    )(*a, w, x, mod)


def _gmlp_kernel(u_ref, v_ref, vg_ref, ws_ref, bs_ref, o_ref):
    width = v_ref.shape[1]
    gw = width // A_GROUPS
    vg = vg_ref[...]
    for c in range(v_ref.shape[0] // A_CHUNK):
        rows = slice(c * A_CHUNK, (c + 1) * A_CHUNK)
        v = v_ref[rows, :].astype(F32)
        vn = v * lax.rsqrt(jnp.mean(v * v, axis=-1, keepdims=True) + EPS)
        vn = (vn * vg).astype(BF16)
        for g in range(A_GROUPS):
            cols = slice(g * gw, (g + 1) * gw)
            sv = jnp.dot(ws_ref[g].astype(BF16), vn[:, cols], preferred_element_type=F32)
            sv = sv + bs_ref[:, g:g + 1]
            o_ref[rows, cols] = (u_ref[rows, cols].astype(F32) * sv).astype(o_ref.dtype)


def _gmlp_core(uv, v_g, w_s, b_s, j, rows_per_step=4 * A_CHUNK):
    m, two_w = uv.shape
    width = two_w // 2
    tr = rows_per_step
    return pl.pallas_call(
        _gmlp_kernel,
        grid=(m // tr,),
        in_specs=[pl.BlockSpec((tr, width), lambda i: (i, 0)),
                  pl.BlockSpec((tr, width), lambda i: (i, 1)),
                  _lead((j,), (1, width), lambda i: (0, 0)),
                  _lead((j,), (A_GROUPS, A_CHUNK, A_CHUNK), lambda i: (0, 0, 0)),
                  _lead((j,), (A_CHUNK, A_GROUPS), lambda i: (0, 0))],
        out_specs=pl.BlockSpec((tr, width), lambda i: (i, 0)),
        out_shape=jax.ShapeDtypeStruct((m, width), BF16),
        compiler_params=_params(1),
        name="gmlp_core",
    )(uv, uv, v_g[:, None, :], w_s, jnp.swapaxes(b_s, 1, 2))


def _ret_kernel(*refs, seq_len, dk, hb, has_state, has_rope, emit_state):
    it = iter(refs)
    q_ref, k_ref, v_ref, g_ref, dec_ref, gn_ref = (next(it) for _ in range(6))
    cos_ref, sin_ref = (next(it), next(it)) if has_rope else (None, None)
    s0_ref = next(it) if has_state else None
    o_ref = next(it)
    st_ref = next(it) if emit_state else None
    q_scr, k_scr, acc_scr, r_scr = (next(it) for _ in range(4))
    chunk = R_CHUNK
    n_chunks = seq_len // chunk
    dv = 2 * dk

    ri = lax.broadcasted_iota(jnp.int32, (chunk, chunk), 0).astype(F32)
    ci = lax.broadcasted_iota(jnp.int32, (chunk, chunk), 1).astype(F32)
    rel = ri - ci

    def head(hh):
        qk_cols = slice(hh * dk, (hh + 1) * dk)
        v_cols = slice(hh * dv, (hh + 1) * dv)
        q = q_ref[:, qk_cols].astype(F32)
        k = k_ref[:, qk_cols].astype(F32)
        if has_rope:
            cos, sin = cos_ref[...], sin_ref[...]

            def rope(t):
                swapped = jnp.concatenate(
                    [pltpu.roll(t[:, s:s + LANES], LANES // 2, 1) for s in range(0, dk, LANES)],
                    axis=1)
                return t * cos + swapped * sin

            q, k = rope(q), rope(k)
        q_scr[:, qk_cols] = q.astype(BF16)
        k_scr[:, qk_cols] = k * (dk ** -0.5)

        consts = []
        for direction in range(2):
            lg = _log_sigmoid(dec_ref[direction, hh])
            if direction == 0:
                dmat = jnp.where(rel >= 0, jnp.exp(jnp.maximum(rel, 0.0) * lg), 0.0)
                zeta = jnp.exp((chunk - 1 - ri) * lg)[:, :1]
                xi = jnp.exp((ri + 1) * lg)[:, :1]
            else:
                dmat = jnp.where(rel <= 0, jnp.exp(jnp.maximum(-rel, 0.0) * lg), 0.0)
                zeta = jnp.exp(ri * lg)[:, :1]
                xi = jnp.exp((chunk - ri) * lg)[:, :1]
            consts.append((dmat, zeta, xi, jnp.exp(chunk * lg)[:, :1]))
            if has_state:
                r_scr[direction, hh] = s0_ref[direction, hh].astype(F32)
            else:
                r_scr[direction, hh] = jnp.zeros((dk, dv), F32)

        def chunk_step(direction, i):
            dmat, zeta, xi, g_chunk = consts[direction]
            rows = slice(i * chunk, (i + 1) * chunk)
            qi = q_scr[rows, qk_cols]
            kf = k_scr[rows, qk_cols]
            vi = v_ref[rows, v_cols]
            s = lax.dot_general(qi, kf.astype(BF16), (((1,), (1,)), ((), ())),
                                preferred_element_type=F32)
            o_inner = jnp.dot((s * dmat).astype(BF16), vi, preferred_element_type=F32)
            o_cross = jnp.dot(qi, r_scr[direction, hh].astype(BF16),
                              preferred_element_type=F32) * xi
            kv = lax.dot_general((kf * zeta).astype(BF16), vi, (((0,), (0,)), ((), ())),
                                 preferred_element_type=F32)
            r_scr[direction, hh] = g_chunk * r_scr[direction, hh] + kv
            acc_scr[direction, rows, v_cols] = o_inner + o_cross

        for step in range(n_chunks):
            chunk_step(0, step)
            chunk_step(1, n_chunks - 1 - step)

        gn = gn_ref[:, v_cols]
        for i in range(n_chunks):
            rows = slice(i * chunk, (i + 1) * chunk)
            o = acc_scr[0, rows, v_cols] + acc_scr[1, rows, v_cols]
            mu = jnp.mean(o, axis=-1, keepdims=True)
            var = jnp.mean(jnp.square(o - mu), axis=-1, keepdims=True)
            on = ((o - mu) * lax.rsqrt(var + EPS)) * gn
            o_ref[rows, v_cols] = (jax.nn.silu(g_ref[rows, v_cols].astype(F32)) * on
                                   ).astype(o_ref.dtype)

    for hh in range(hb):
        head(hh)
    if emit_state:
        st_ref[...] = r_scr[...]


def _retention_core(proj, decay, gn_g, j, *, row0, n_seq, seq_len, state, rope, emit_state,
                    hb=2):
    heads = R_HEADS
    dk = proj.shape[1] // (6 * heads)
    dv = 2 * dk
    blk0 = row0 // seq_len
    hg = heads // hb
    dec = jnp.broadcast_to(decay[j][:, :, None, None], (2, heads, 1, R_CHUNK))
    in_specs = [pl.BlockSpec((seq_len, hb * dk), lambda b, h: (blk0 + b, h)),
                pl.BlockSpec((seq_len, hb * dk), lambda b, h: (blk0 + b, hg + h)),
                pl.BlockSpec((seq_len, hb * dv), lambda b, h: (blk0 + b, hg + h)),
                pl.BlockSpec((seq_len, hb * dv), lambda b, h: (blk0 + b, 2 * hg + h)),
                pl.BlockSpec((2, hb, 1, R_CHUNK), lambda b, h: (0, h, 0, 0)),
                _lead((j,), (1, hb * dv), lambda b, h: (0, h))]
    args = [proj, proj, proj, proj, dec, gn_g[:, None, :]]
    if rope is not None:
        in_specs += [pl.BlockSpec((seq_len, dk), lambda b, h: (0, 0))] * 2
        args += list(rope)
    if state is not None:
        in_specs.append(pl.BlockSpec((None, None, 2, hb, dk, dv),
                                     lambda b, h: (b, j, 0, h, 0, 0)))
        args.append(state)
    out_specs = [pl.BlockSpec((seq_len, hb * dv), lambda b, h: (b, h))]
    out_shape = [jax.ShapeDtypeStruct((n_seq * seq_len, heads * dv), BF16)]
    if emit_state:
        out_specs.append(pl.BlockSpec((None, 2, hb, dk, dv), lambda b, h: (b, 0, h, 0, 0)))
        out_shape.append(jax.ShapeDtypeStruct((n_seq, 2, heads, dk, dv), F32))
    body = functools.partial(_ret_kernel, seq_len=seq_len, dk=dk, hb=hb,
                             has_state=state is not None, has_rope=rope is not None,
                             emit_state=emit_state)
    return pl.pallas_call(
        body,
        grid=(n_seq, hg),
        in_specs=in_specs,
        out_specs=out_specs,
        out_shape=out_shape,
        scratch_shapes=[pltpu.VMEM((seq_len, hb * dk), BF16),
                        pltpu.VMEM((seq_len, hb * dk), F32),
                        pltpu.VMEM((2, seq_len, hb * dv), F32),
                        pltpu.VMEM((2, hb, dk, dv), F32)],
        compiler_params=_params(2),
        name="retention_core",
    )(*args)


def _rope_tables(seq_len, dk):
    nf = dk // 4
    inv = ROPE_BASE ** (-jnp.arange(nf, dtype=F32) / nf)
    t = jnp.arange(seq_len)
    tabs = []
    for p in (t // GRID_W, t % GRID_W):
        ang = p.astype(F32)[:, None] * inv
        tabs.append((jnp.cos(ang), jnp.sin(ang)))
    cos = jnp.concatenate([tabs[0][0], tabs[0][0], tabs[1][0], tabs[1][0]], axis=1)
    sin = jnp.concatenate([-tabs[0][1], tabs[0][1], -tabs[1][1], tabs[1][1]], axis=1)
    return cos, sin


def _lru_kernel(*refs, seq_len, n_blk, has_state):
    it = iter(refs)
    y_ref, x_ref, cw_ref, cb_ref, gw_ref, gb_ref, lam_ref = (next(it) for _ in range(7))
    h0_ref = next(it) if has_state else None
    o_ref, st_ref, ab_scr, run_scr, h_scr = (next(it) for _ in range(5))
    n = seq_len
    row = lax.broadcasted_iota(jnp.int32, (n, LRU_BLOCK), 0)

    def shifted(v, d, fill):
        if d == 0:
            return v
        if d > 0:
            return jnp.where(row >= d, pltpu.roll(v, d, 0), fill)
        return jnp.where(row < n + d, pltpu.roll(v, n + d, 0), fill)

    seg = n // SUBLANES
    pitch = seg + SEG_PAD
    n_slabs = x_ref.shape[1] // LANES
    slabs_per_block = LRU_BLOCK // LANES
    conv_w = cw_ref.shape[0]
    pad_l = conv_w // 2
    for blk in range(n_blk):
        bc = slice(blk * LRU_BLOCK, (blk + 1) * LRU_BLOCK)
        x = x_ref[:, bc].astype(F32)
        xc = shifted(x, pad_l, 0.0) * cw_ref[0:1, bc]
        for t in range(1, conv_w):
            xc = xc + shifted(x, pad_l - t, 0.0) * cw_ref[t:t + 1, bc]
        xc = xc + cb_ref[:, bc]
        xcb = xc.astype(BF16)
        for direction in range(2):
            gates = [jax.nn.sigmoid(
                jnp.dot(xcb, gw_ref[direction, gi, blk].astype(BF16), preferred_element_type=F32)
                + gb_ref[direction, gi:gi + 1, bc]) for gi in range(2)]
            log_a = (LRU_C * gates[0]) * _log_sigmoid(lam_ref[direction:direction + 1, bc])
            a = jnp.exp(log_a)
            b = (jnp.sqrt(-jnp.tanh(log_a) * (a * a + 1.0)) * gates[1]) * xc
            for sl in range(slabs_per_block):
                slab = blk * slabs_per_block + sl
                cols = slice(sl * LANES, (sl + 1) * LANES)
                for s in range(SUBLANES):
                    src, dst = slice(s * seg, (s + 1) * seg), slice(s * pitch, s * pitch + seg)
                    ab_scr[direction, 0, slab, dst, :] = a[src, cols]
                    ab_scr[direction, 1, slab, dst, :] = b[src, cols]

    def step_rows(i):
        return pl.ds(i, SUBLANES, stride=pitch)

    chains = [(d, sl) for d in range(2) for sl in range(n_slabs)]

    def scan_body(i, carry):
        out = []
        for (d, sl), (a_run, h_run) in zip(chains, carry):
            rows = step_rows(i if d == 0 else seg - 1 - i)
            a_i = ab_scr[d, 0, sl, rows, :]
            h_run = a_i * h_run + ab_scr[d, 1, sl, rows, :]
            a_run = a_i * a_run
            run_scr[d, 0, sl, rows, :] = a_run
            run_scr[d, 1, sl, rows, :] = h_run
            out.append((a_run, h_run))
        return tuple(out)

    init = tuple((jnp.ones((SUBLANES, LANES), F32), jnp.zeros((SUBLANES, LANES), F32))
                 for _ in chains)
    ends = lax.fori_loop(0, seg, scan_body, init, unroll=SCAN_UNROLL)

    entering = []
    for (d, sl), (a_end, h_end) in zip(chains, ends):
        cols = slice(sl * LANES, (sl + 1) * LANES)
        c = h0_ref[d:d + 1, cols] if has_state else jnp.zeros((1, LANES), F32)
        rows = [None] * SUBLANES
        for s in (range(SUBLANES) if d == 0 else range(SUBLANES - 1, -1, -1)):
            rows[s] = c
            c = h_end[s:s + 1, :] + a_end[s:s + 1, :] * c
        st_ref[d:d + 1, cols] = c
        entering.append(jnp.concatenate(rows, axis=0))

    def fix_body(i, carry):
        rows = step_rows(i)
        for sl in range(n_slabs):
            h_f, h_b = (run_scr[d, 1, sl, rows, :]
                        + run_scr[d, 0, sl, rows, :] * entering[chains.index((d, sl))]
                        for d in range(2))
            h_scr[sl, rows, :] = h_f + h_b
        return carry

    lax.fori_loop(0, seg, fix_body, 0, unroll=SCAN_UNROLL)

    for sl in range(n_slabs):
        cols = slice(sl * LANES, (sl + 1) * LANES)
        for s in range(SUBLANES):
            src, dst = slice(s * pitch, s * pitch + seg), slice(s * seg, (s + 1) * seg)
            o_ref[dst, cols] = (h_scr[sl, src, :]
                                * jax.nn.gelu(y_ref[dst, cols].astype(F32))).astype(o_ref.dtype)


def _lru_core(proj, conv_w, conv_b, gate_w, gate_b, lam, j, *, row0, n_seq, seq_len, state,
              n_blk=2):
    width = proj.shape[1] // 2
    tw = n_blk * LRU_BLOCK
    nb = width // tw
    blk0 = row0 // seq_len
    in_specs = [pl.BlockSpec((seq_len, tw), lambda b, c: (blk0 + b, c)),
                pl.BlockSpec((seq_len, tw), lambda b, c: (blk0 + b, nb + c)),
                _lead((j,), (conv_w.shape[1], tw), lambda b, c: (0, c)),
                _lead((j,), (1, tw), lambda b, c: (0, c)),
                _lead((j,), (2, 2, n_blk, LRU_BLOCK, LRU_BLOCK), lambda b, c: (0, 0, c, 0, 0)),
                _lead((j,), (2, 2, tw), lambda b, c: (0, 0, c)),
                _lead((j,), (2, tw), lambda b, c: (0, c))]
    args = [proj, proj, conv_w, conv_b[:, None, :], gate_w, gate_b, lam]
    if state is not None:
        in_specs.append(pl.BlockSpec((None, None, 2, tw), lambda b, c: (b, j, 0, c)))
        args.append(state)
    body = functools.partial(_lru_kernel, seq_len=seq_len, n_blk=n_blk,
                             has_state=state is not None)
    n_slabs = tw // LANES
    scr_rows = SUBLANES * (seq_len // SUBLANES + SEG_PAD)
    return pl.pallas_call(
        body,
        grid=(n_seq, nb),
        in_specs=in_specs,
        out_specs=[pl.BlockSpec((seq_len, tw), lambda b, c: (b, c)),
                   pl.BlockSpec((None, 2, tw), lambda b, c: (b, 0, c))],
        out_shape=[jax.ShapeDtypeStruct((n_seq * seq_len, width), BF16),
                   jax.ShapeDtypeStruct((n_seq, 2, width), F32)],
        scratch_shapes=[pltpu.VMEM((2, 2, n_slabs, scr_rows, LANES), F32),
                        pltpu.VMEM((2, 2, n_slabs, scr_rows, LANES), F32),
                        pltpu.VMEM((n_slabs, scr_rows, LANES), F32)],
        compiler_params=_params(2),
        name="lru_core",
    )(*args)


def _norm_kernel(x_ref, g_ref, o_ref):
    x = x_ref[...]
    o_ref[...] = (x * lax.rsqrt(jnp.mean(x * x, axis=-1, keepdims=True) + EPS)) * g_ref[...]


def _final_norm(x, gain, row0, rows, tr=512):
    d = x.shape[1]
    blk0 = row0 // tr
    return pl.pallas_call(
        _norm_kernel,
        grid=(rows // tr,),
        in_specs=[pl.BlockSpec((tr, d), lambda i: (blk0 + i, 0)),
                  pl.BlockSpec((1, d), lambda i: (0, 0))],
        out_specs=pl.BlockSpec((tr, d), lambda i: (i, 0)),
        out_shape=jax.ShapeDtypeStruct((rows, d), F32),
        compiler_params=_params(1),
        name="final_norm",
    )(x, gain.reshape(1, d))


def kernel(x_prompt, x_sample, state_ret, state_lru, c, c_ctx, norm_g, final_g, w_mod, b_mod,
           ffn_w_in, ffn_w_out, a_w_in, a_v_g, a_w_s, a_b_s, a_w_out, r_w_in, r_decay, r_gn_g,
           r_w_out, l_w_in, l_conv_w, l_conv_b, l_gate_w, l_gate_b, l_lambda, l_w_out):
    batch, seq, d = x_prompt.shape
    dec_batch, dec_seq, _ = x_sample.shape
    depth = w_mod.shape[0]
    m_prompt, m_sample = batch * seq, dec_batch * dec_seq
    assert m_prompt % TM == 0 and dec_seq % TM == 0 and dec_batch + 1 <= MOD_ROWS
    prompt_tiles, tiles_per_sample = m_prompt // TM, dec_seq // TM

    def seg(i):
        return jnp.maximum(i - prompt_tiles, -1) // tiles_per_sample + 1

    x = jnp.concatenate([x_prompt.reshape(m_prompt, d), x_sample.reshape(m_sample, d)], axis=0)
    cond = jnp.concatenate(
        [c_ctx[None, :], c, jnp.zeros((MOD_ROWS - 1 - dec_batch, d), F32)], axis=0)
    mod_all = _modulation(cond, w_mod, b_mod).reshape(depth, MOD_ROWS, N_MOD, d)
    rope = _rope_tables(dec_seq, d // R_HEADS)

    def ffn(x, l, s, mod, tn=512):
        hid, w_out = _project(x, norm_g[l, 2 * s], mod, ffn_w_in, (l, s), seg, shift_idx=6 * s,
                              scale_idx=6 * s + 1, act="swiglu", tn=tn,
                              cast=(ffn_w_out, (l, s)))
        return _out_project(hid, w_out, x, mod, seg, gate_idx=6 * s + 2, coef=0.5)

    ret_states, lru_states = [], []
    for l in range(depth):
        kind, j = l % N_MIXERS, l // N_MIXERS
        mod = mod_all[l]
        x = ffn(x, l, 0, mod)
        pre = dict(shift_idx=3, scale_idx=4, tn=1024)
        if kind == 0:
            uv, w_out = _project(x, norm_g[l, 1], mod, a_w_in, (j,), seg, act="gelu",
                                 cast=(a_w_out, (j,)), **pre)
            mixed = _gmlp_core(uv, a_v_g, a_w_s, a_b_s, j)
        elif kind == 1:
            proj, w_out = _project(x, norm_g[l, 1], mod, r_w_in, (j,), seg, act="none",
                                   cast=(r_w_out, (j,)), **pre)
            o_p, st = _retention_core(proj, r_decay, r_gn_g, j, row0=0, n_seq=batch,
                                      seq_len=seq, state=None, rope=None, emit_state=True,
                                      hb=4)
            (o_s,) = _retention_core(proj, r_decay, r_gn_g, j, row0=m_prompt, n_seq=dec_batch,
                                     seq_len=dec_seq, state=state_ret, rope=rope,
                                     emit_state=False)
            mixed = (o_p, o_s)
            ret_states.append(st)
        else:
            proj, w_out = _project(x, norm_g[l, 1], mod, l_w_in, (j,), seg, act="none",
                                   cast=(l_w_out, (j,)), **pre)
            o_p, st = _lru_core(proj, l_conv_w, l_conv_b, l_gate_w, l_gate_b, l_lambda, j, row0=0,
                                n_seq=batch, seq_len=seq, state=None)
            o_s, _ = _lru_core(proj, l_conv_w, l_conv_b, l_gate_w, l_gate_b, l_lambda, j,
                               row0=m_prompt, n_seq=dec_batch, seq_len=dec_seq, state=state_lru)
            mixed = (o_p, o_s)
            lru_states.append(st)
        x = _out_project(mixed, w_out, x, mod, seg, gate_idx=5, coef=1.0)
        x = ffn(x, l, 1, mod)

    y_prompt = _final_norm(x, final_g, 0, m_prompt).reshape(batch, seq, d)
    y_sample = _final_norm(x, final_g, m_prompt, m_sample).reshape(dec_batch, dec_seq, d)
    new_state_ret = jnp.stack(ret_states, axis=1)
    new_state_lru = jnp.stack(lru_states, axis=1)
    return (y_prompt, y_sample, new_state_ret, new_state_lru)
```

```python
import functools

import jax
import jax.numpy as jnp
from jax import lax
from jax.experimental import pallas as pl
from jax.experimental.pallas import tpu as pltpu

F32 = jnp.float32
BF16 = jnp.bfloat16

EPS = 1e-6
N_MIXERS = 3
N_MOD = 9
GRID_W = 64
A_CHUNK = 128
A_GROUPS = 8
R_HEADS = 8
R_CHUNK = 256
ROPE_BASE = 10000.0
LRU_BLOCK = 256
LRU_C = 8.0

LANES = 128
SUBLANES = 8
MXU_N = 256
SEG_PAD = 4
SCAN_UNROLL = 8
MOD_ROWS = SUBLANES
TM = 1024
ROW_CHUNK = 16
CAST_ROWS = 256
MIX_CAST_ROWS = 32
VMEM_LIMIT = 56 * 1024 * 1024
OUT_VMEM_BUDGET = 52 * 1024 * 1024
PROJ_VMEM_BUDGET = 52 * 1024 * 1024


def _params(n_grid):
    return pltpu.CompilerParams(dimension_semantics=("arbitrary",) * n_grid,
                                vmem_limit_bytes=VMEM_LIMIT)


def _lead(lead, block, index_map):
    return pl.BlockSpec((None,) * len(lead) + tuple(block),
                        lambda *g: tuple(lead) + tuple(index_map(*g)))


def _log_sigmoid(x):
    return jnp.minimum(x, 0.0) - jnp.log1p(jnp.exp(-jnp.abs(x)))


def _mod_kernel(c_ref, w_ref, b_ref, o_ref):
    s = jax.nn.silu(c_ref[...]).astype(BF16)
    o_ref[...] = jnp.dot(s, w_ref[...].astype(BF16), preferred_element_type=F32) + b_ref[...]


def _modulation(cond, w_mod, b_mod, tn=2048):
    depth, d, n = w_mod.shape
    return pl.pallas_call(
        _mod_kernel,
        grid=(depth, n // tn),
        in_specs=[pl.BlockSpec((MOD_ROWS, d), lambda l, j: (0, 0)),
                  pl.BlockSpec((None, d, tn), lambda l, j: (l, 0, j)),
                  pl.BlockSpec((None, 1, tn), lambda l, j: (l, 0, j))],
        out_specs=pl.BlockSpec((None, MOD_ROWS, tn), lambda l, j: (l, 0, j)),
        out_shape=jax.ShapeDtypeStruct((depth, MOD_ROWS, n), F32),
        compiler_params=_params(2),
        name="modulation",
    )(cond, w_mod, b_mod.reshape(depth, 1, n))


def _round_chunk(src_ref, dst_ref):
    dst_ref[...] = src_ref[...].astype(BF16)


def _proj_kernel(x_ref, g_ref, mod_ref, *refs, n_w, cast_steps, shift_idx, scale_idx, act):
    n_c = len(cast_steps)
    w_refs, wc_refs = refs[:n_w], refs[n_w:n_w + n_c]
    o_ref, wb_refs, h_ref = refs[n_w + n_c], refs[n_w + n_c + 1:-1], refs[-1]

    @pl.when(pl.program_id(1) == 0)
    def _prologue():
        gain = g_ref[...]
        scale = 1.0 + mod_ref[scale_idx:scale_idx + 1, :]
        shift = mod_ref[shift_idx:shift_idx + 1, :]

        def rows_body(r, carry):
            rows = pl.ds(pl.multiple_of(r * ROW_CHUNK, ROW_CHUNK), ROW_CHUNK)
            x = x_ref[rows, :]
            y = x * lax.rsqrt(jnp.mean(x * x, axis=-1, keepdims=True) + EPS)
            h_ref[rows, :] = ((y * gain) * scale + shift).astype(h_ref.dtype)
            return carry

        lax.fori_loop(0, x_ref.shape[0] // ROW_CHUNK, rows_body, 0, unroll=8)

    step = pl.program_id(0) * pl.num_programs(1) + pl.program_id(1)
    for wc_ref, wb_ref, n_cast in zip(wc_refs, wb_refs, cast_steps):
        pl.when(step < n_cast)(functools.partial(_round_chunk, wc_ref, wb_ref))

    h = h_ref[...]
    for c in range(0, o_ref.shape[1], MXU_N):
        cols = slice(c, c + MXU_N)
        ys = [jnp.dot(h, w[:, cols].astype(BF16), preferred_element_type=F32) for w in w_refs]
        if act == "swiglu":
            out = jax.nn.silu(ys[0]) * ys[1]
        elif act == "gelu":
            out = jax.nn.gelu(ys[0])
        else:
            out = ys[0]
        o_ref[:, cols] = out.astype(o_ref.dtype)


def _proj_tile_n(d, n_out, n_w, w_itemsize, cast_bytes):
    fixed = 2 * TM * d * 4 + TM * d * 2 + cast_bytes
    for k in range(n_out // MXU_N, 0, -1):
        tn = k * MXU_N
        blocks = fixed + 2 * (n_w * d * tn * w_itemsize + TM * tn * 2)
        if n_out % tn == 0 and blocks <= PROJ_VMEM_BUDGET:
            return tn
    raise ValueError("projection tiles do not fit VMEM")


def _project(x, gain, mod, w, lead, seg, *, shift_idx, scale_idx, act, casts=()):
    m, d = x.shape
    n_w = 2 if act == "swiglu" else 1
    n_out = w.shape[-1] // n_w
    cast_bytes = sum(2 * rows * wn.shape[-1] * (wn.dtype.itemsize + 2) for wn, _, rows in casts)
    tn = _proj_tile_n(d, n_out, n_w, w.dtype.itemsize, cast_bytes)
    grid = (m // TM, n_out // tn)
    w_specs = [_lead(lead, (d, tn), lambda i, j: (0, j))]
    if act == "swiglu":
        w_specs.append(_lead(lead, (d, tn), lambda i, j: (0, j + grid[1])))
    in_specs = [pl.BlockSpec((TM, d), lambda i, j: (i, 0)),
                pl.BlockSpec((1, d), lambda i, j: (0, 0)),
                pl.BlockSpec((None, N_MOD, d), lambda i, j: (seg(i), 0, 0))] + w_specs
    args = [x, gain.reshape(1, d), mod] + [w] * n_w
    out_specs = [pl.BlockSpec((TM, tn), lambda i, j: (i, j))]
    out_shape = [jax.ShapeDtypeStruct((m, n_out), BF16)]
    cast_steps = []
    for w_next, lead_next, rows in casts:
        k_next, d_next = w_next.shape[-2:]
        n_cast = k_next // rows
        assert k_next % rows == 0 and n_cast <= grid[0] * grid[1]

        def chunk(i, j, n_cast=n_cast):
            return (jnp.minimum(i * grid[1] + j, n_cast - 1), 0)

        in_specs.append(_lead(lead_next, (rows, d_next), chunk))
        args.append(w_next)
        out_specs.append(pl.BlockSpec((rows, d_next), chunk))
        out_shape.append(jax.ShapeDtypeStruct((k_next, d_next), BF16))
        cast_steps.append(n_cast)
    body = functools.partial(_proj_kernel, n_w=n_w, cast_steps=tuple(cast_steps),
                             shift_idx=shift_idx, scale_idx=scale_idx, act=act)
    return pl.pallas_call(
        body,
        grid=grid,
        in_specs=in_specs,
        out_specs=out_specs,
        out_shape=out_shape,
        scratch_shapes=[pltpu.VMEM((TM, d), BF16)],
        compiler_params=_params(2),
        name="project_" + act,
    )(*args)


def _out_kernel(*refs, n_a, split, gate_idx, coef):
    a_refs = refs[:n_a]
    w_ref, x_ref, mod_ref, o_ref = refs[n_a:]

    def update(a_ref):
        y = jnp.dot(a_ref[...], w_ref[...], preferred_element_type=F32)
        gate = mod_ref[gate_idx:gate_idx + 1, :]
        if coef != 1.0:
            gate = coef * gate
        o_ref[...] = x_ref[...] + gate * y

    if n_a == 1:
        update(a_refs[0])
    else:
        pl.when(pl.program_id(0) < split)(functools.partial(update, a_refs[0]))
        pl.when(pl.program_id(0) >= split)(functools.partial(update, a_refs[1]))


def _out_tile_n(n_a, k, d):
    for tn in (2048, 1024, 512, 256):
        blocks = 2 * (n_a * TM * k * 2 + k * tn * 2 + 2 * TM * tn * 4) + TM * tn * 4
        if d % tn == 0 and blocks <= OUT_VMEM_BUDGET:
            return tn
    raise ValueError("out_project tiles do not fit VMEM")


def _out_project(a, w, x, mod, seg, *, gate_idx, coef):
    m, d = x.shape
    assert w.dtype == BF16
    tn = _out_tile_n(len(a) if isinstance(a, tuple) else 1, w.shape[0], d)
    if isinstance(a, tuple):
        split = a[0].shape[0] // TM
        k = a[0].shape[1]
        a_specs = [pl.BlockSpec((TM, k), lambda i, j: (jnp.minimum(i, split - 1), 0)),
                   pl.BlockSpec((TM, k), lambda i, j: (jnp.maximum(i - split, 0), 0))]
    else:
        split, k = 0, a.shape[1]
        a_specs = [pl.BlockSpec((TM, k), lambda i, j: (i, 0))]
        a = (a,)
    body = functools.partial(_out_kernel, n_a=len(a), split=split, gate_idx=gate_idx, coef=coef)
    return pl.pallas_call(
        body,
        grid=(m // TM, d // tn),
        in_specs=a_specs + [pl.BlockSpec((k, tn), lambda i, j: (0, j)),
                            pl.BlockSpec((TM, tn), lambda i, j: (i, j)),
                            pl.BlockSpec((None, N_MOD, tn), lambda i, j: (seg(i), 0, j))],
        out_specs=pl.BlockSpec((TM, tn), lambda i, j: (i, j)),
        out_shape=jax.ShapeDtypeStruct((m, d), F32),
        compiler_params=_params(2),
        name="out_project",
    )(*a, w, x, mod)


def _gmlp_kernel(u_ref, v_ref, vg_ref, ws_ref, bs_ref, o_ref):
    width = v_ref.shape[1]
    gw = width // A_GROUPS
    vg = vg_ref[...]
    for c in range(v_ref.shape[0] // A_CHUNK):
        rows = slice(c * A_CHUNK, (c + 1) * A_CHUNK)
        v = v_ref[rows, :].astype(F32)
        vn = v * lax.rsqrt(jnp.mean(v * v, axis=-1, keepdims=True) + EPS)
        vn = (vn * vg).astype(BF16)
        for g in range(A_GROUPS):
            cols = slice(g * gw, (g + 1) * gw)
            sv = jnp.dot(ws_ref[g].astype(BF16), vn[:, cols], preferred_element_type=F32)
            sv = sv + bs_ref[:, g:g + 1]
            o_ref[rows, cols] = (u_ref[rows, cols].astype(F32) * sv).astype(o_ref.dtype)


def _gmlp_core(uv, v_g, w_s, b_s, j, rows_per_step=4 * A_CHUNK):
    m, two_w = uv.shape
    width = two_w // 2
    tr = rows_per_step
    return pl.pallas_call(
        _gmlp_kernel,
        grid=(m // tr,),
        in_specs=[pl.BlockSpec((tr, width), lambda i: (i, 0)),
                  pl.BlockSpec((tr, width), lambda i: (i, 1)),
                  _lead((j,), (1, width), lambda i: (0, 0)),
                  _lead((j,), (A_GROUPS, A_CHUNK, A_CHUNK), lambda i: (0, 0, 0)),
                  _lead((j,), (A_CHUNK, A_GROUPS), lambda i: (0, 0))],
        out_specs=pl.BlockSpec((tr, width), lambda i: (i, 0)),
        out_shape=jax.ShapeDtypeStruct((m, width), BF16),
        compiler_params=_params(1),
        name="gmlp_core",
    )(uv, uv, v_g[:, None, :], w_s, jnp.swapaxes(b_s, 1, 2))


def _ret_kernel(*refs, seq_len, dk, hb, has_state, has_rope, emit_state):
    it = iter(refs)
    q_ref, k_ref, v_ref, g_ref, dec_ref, gn_ref = (next(it) for _ in range(6))
    cos_ref, sin_ref = (next(it), next(it)) if has_rope else (None, None)
    s0_ref = next(it) if has_state else None
    o_ref = next(it)
    st_ref = next(it) if emit_state else None
    q_scr, k_scr, acc_scr, r_scr = (next(it) for _ in range(4))
    chunk = R_CHUNK
    n_chunks = seq_len // chunk
    dv = 2 * dk

    ri = lax.broadcasted_iota(jnp.int32, (chunk, chunk), 0).astype(F32)
    ci = lax.broadcasted_iota(jnp.int32, (chunk, chunk), 1).astype(F32)
    rel = ri - ci

    def head(hh):
        qk_cols = slice(hh * dk, (hh + 1) * dk)
        v_cols = slice(hh * dv, (hh + 1) * dv)
        q = q_ref[:, qk_cols].astype(F32)
        k = k_ref[:, qk_cols].astype(F32)
        if has_rope:
            cos, sin = cos_ref[...], sin_ref[...]

            def rope(t):
                swapped = jnp.concatenate(
                    [pltpu.roll(t[:, s:s + LANES], LANES // 2, 1) for s in range(0, dk, LANES)],
                    axis=1)
                return t * cos + swapped * sin

            q, k = rope(q), rope(k)
        q_scr[:, qk_cols] = q.astype(BF16)
        k_scr[:, qk_cols] = k * (dk ** -0.5)

        consts = []
        for direction in range(2):
            lg = _log_sigmoid(dec_ref[direction, hh])
            if direction == 0:
                dmat = jnp.where(rel >= 0, jnp.exp(jnp.maximum(rel, 0.0) * lg), 0.0)
                zeta = jnp.exp((chunk - 1 - ri) * lg)[:, :1]
                xi = jnp.exp((ri + 1) * lg)[:, :1]
            else:
                dmat = jnp.where(rel <= 0, jnp.exp(jnp.maximum(-rel, 0.0) * lg), 0.0)
                zeta = jnp.exp(ri * lg)[:, :1]
                xi = jnp.exp((chunk - ri) * lg)[:, :1]
            consts.append((dmat, zeta, xi, jnp.exp(chunk * lg)[:, :1]))
            if has_state:
                r_scr[direction, hh] = s0_ref[direction, hh].astype(F32)
            else:
                r_scr[direction, hh] = jnp.zeros((dk, dv), F32)

        def chunk_step(direction, i):
            dmat, zeta, xi, g_chunk = consts[direction]
            rows = slice(i * chunk, (i + 1) * chunk)
            qi = q_scr[rows, qk_cols]
            kf = k_scr[rows, qk_cols]
            vi = v_ref[rows, v_cols]
            s = lax.dot_general(qi, kf.astype(BF16), (((1,), (1,)), ((), ())),
                                preferred_element_type=F32)
            o_inner = jnp.dot((s * dmat).astype(BF16), vi, preferred_element_type=F32)
            o_cross = jnp.dot(qi, r_scr[direction, hh].astype(BF16),
                              preferred_element_type=F32) * xi
            kv = lax.dot_general((kf * zeta).astype(BF16), vi, (((0,), (0,)), ((), ())),
                                 preferred_element_type=F32)
            r_scr[direction, hh] = g_chunk * r_scr[direction, hh] + kv
            acc_scr[direction, rows, v_cols] = o_inner + o_cross

        for step in range(n_chunks):
            chunk_step(0, step)
            chunk_step(1, n_chunks - 1 - step)

        gn = gn_ref[:, v_cols]
        for i in range(n_chunks):
            rows = slice(i * chunk, (i + 1) * chunk)
            o = acc_scr[0, rows, v_cols] + acc_scr[1, rows, v_cols]
            mu = jnp.mean(o, axis=-1, keepdims=True)
            var = jnp.mean(jnp.square(o - mu), axis=-1, keepdims=True)
            on = ((o - mu) * lax.rsqrt(var + EPS)) * gn
            o_ref[rows, v_cols] = (jax.nn.silu(g_ref[rows, v_cols].astype(F32)) * on
                                   ).astype(o_ref.dtype)

    for hh in range(hb):
        head(hh)
    if emit_state:
        st_ref[...] = r_scr[...]


def _retention_core(proj, decay, gn_g, j, *, row0, n_seq, seq_len, state, rope, emit_state,
                    hb=2):
    heads = R_HEADS
    dk = proj.shape[1] // (6 * heads)
    dv = 2 * dk
    blk0 = row0 // seq_len
    hg = heads // hb
    dec = jnp.broadcast_to(decay[j][:, :, None, None], (2, heads, 1, R_CHUNK))
    in_specs = [pl.BlockSpec((seq_len, hb * dk), lambda b, h: (blk0 + b, h)),
                pl.BlockSpec((seq_len, hb * dk), lambda b, h: (blk0 + b, hg + h)),
                pl.BlockSpec((seq_len, hb * dv), lambda b, h: (blk0 + b, hg + h)),
                pl.BlockSpec((seq_len, hb * dv), lambda b, h: (blk0 + b, 2 * hg + h)),
                pl.BlockSpec((2, hb, 1, R_CHUNK), lambda b, h: (0, h, 0, 0)),
                _lead((j,), (1, hb * dv), lambda b, h: (0, h))]
    args = [proj, proj, proj, proj, dec, gn_g[:, None, :]]
    if rope is not None:
        in_specs += [pl.BlockSpec((seq_len, dk), lambda b, h: (0, 0))] * 2
        args += list(rope)
    if state is not None:
        in_specs.append(pl.BlockSpec((None, None, 2, hb, dk, dv),
                                     lambda b, h: (b, j, 0, h, 0, 0)))
        args.append(state)
    out_specs = [pl.BlockSpec((seq_len, hb * dv), lambda b, h: (b, h))]
    out_shape = [jax.ShapeDtypeStruct((n_seq * seq_len, heads * dv), BF16)]
    if emit_state:
        out_specs.append(pl.BlockSpec((None, 2, hb, dk, dv), lambda b, h: (b, 0, h, 0, 0)))
        out_shape.append(jax.ShapeDtypeStruct((n_seq, 2, heads, dk, dv), F32))
    body = functools.partial(_ret_kernel, seq_len=seq_len, dk=dk, hb=hb,
                             has_state=state is not None, has_rope=rope is not None,
                             emit_state=emit_state)
    return pl.pallas_call(
        body,
        grid=(n_seq, hg),
        in_specs=in_specs,
        out_specs=out_specs,
        out_shape=out_shape,
        scratch_shapes=[pltpu.VMEM((seq_len, hb * dk), BF16),
                        pltpu.VMEM((seq_len, hb * dk), F32),
                        pltpu.VMEM((2, seq_len, hb * dv), F32),
                        pltpu.VMEM((2, hb, dk, dv), F32)],
        compiler_params=_params(2),
        name="retention_core",
    )(*args)


def _rope_tables(seq_len, dk):
    nf = dk // 4
    inv = ROPE_BASE ** (-jnp.arange(nf, dtype=F32) / nf)
    t = jnp.arange(seq_len)
    tabs = []
    for p in (t // GRID_W, t % GRID_W):
        ang = p.astype(F32)[:, None] * inv
        tabs.append((jnp.cos(ang), jnp.sin(ang)))
    cos = jnp.concatenate([tabs[0][0], tabs[0][0], tabs[1][0], tabs[1][0]], axis=1)
    sin = jnp.concatenate([-tabs[0][1], tabs[0][1], -tabs[1][1], tabs[1][1]], axis=1)
    return cos, sin


def _lru_kernel(*refs, seq_len, n_blk, has_state):
    it = iter(refs)
    y_ref, x_ref, cw_ref, cb_ref, gw_ref, gb_ref, lam_ref = (next(it) for _ in range(7))
    h0_ref = next(it) if has_state else None
    o_ref, st_ref, ab_scr, run_scr, h_scr = (next(it) for _ in range(5))
    n = seq_len
    row = lax.broadcasted_iota(jnp.int32, (n, LRU_BLOCK), 0)

    def shifted(v, d, fill):
        if d == 0:
            return v
        if d > 0:
            return jnp.where(row >= d, pltpu.roll(v, d, 0), fill)
        return jnp.where(row < n + d, pltpu.roll(v, n + d, 0), fill)

    seg = n // SUBLANES
    pitch = seg + SEG_PAD
    n_slabs = x_ref.shape[1] // LANES
    slabs_per_block = LRU_BLOCK // LANES
    conv_w = cw_ref.shape[0]
    pad_l = conv_w // 2
    for blk in range(n_blk):
        bc = slice(blk * LRU_BLOCK, (blk + 1) * LRU_BLOCK)
        x = x_ref[:, bc].astype(F32)
        xc = shifted(x, pad_l, 0.0) * cw_ref[0:1, bc]
        for t in range(1, conv_w):
            xc = xc + shifted(x, pad_l - t, 0.0) * cw_ref[t:t + 1, bc]
        xc = xc + cb_ref[:, bc]
        xcb = xc.astype(BF16)
        for direction in range(2):
            gates = [jax.nn.sigmoid(
                jnp.dot(xcb, gw_ref[direction, gi, blk].astype(BF16), preferred_element_type=F32)
                + gb_ref[direction, gi:gi + 1, bc]) for gi in range(2)]
            log_a = (LRU_C * gates[0]) * _log_sigmoid(lam_ref[direction:direction + 1, bc])
            a = jnp.exp(log_a)
            b = (jnp.sqrt(-jnp.tanh(log_a) * (a * a + 1.0)) * gates[1]) * xc
            for sl in range(slabs_per_block):
                slab = blk * slabs_per_block + sl
                cols = slice(sl * LANES, (sl + 1) * LANES)
                for s in range(SUBLANES):
                    src, dst = slice(s * seg, (s + 1) * seg), slice(s * pitch, s * pitch + seg)
                    ab_scr[direction, 0, slab, dst, :] = a[src, cols]
                    ab_scr[direction, 1, slab, dst, :] = b[src, cols]

    def step_rows(i):
        return pl.ds(i, SUBLANES, stride=pitch)

    chains = [(d, sl) for d in range(2) for sl in range(n_slabs)]

    def scan_body(i, carry):
        out = []
        for (d, sl), (a_run, h_run) in zip(chains, carry):
            rows = step_rows(i if d == 0 else seg - 1 - i)
            a_i = ab_scr[d, 0, sl, rows, :]
            h_run = a_i * h_run + ab_scr[d, 1, sl, rows, :]
            a_run = a_i * a_run
            run_scr[d, 0, sl, rows, :] = a_run
            run_scr[d, 1, sl, rows, :] = h_run
            out.append((a_run, h_run))
        return tuple(out)

    init = tuple((jnp.ones((SUBLANES, LANES), F32), jnp.zeros((SUBLANES, LANES), F32))
                 for _ in chains)
    ends = lax.fori_loop(0, seg, scan_body, init, unroll=SCAN_UNROLL)

    entering = []
    for (d, sl), (a_end, h_end) in zip(chains, ends):
        cols = slice(sl * LANES, (sl + 1) * LANES)
        c = h0_ref[d:d + 1, cols] if has_state else jnp.zeros((1, LANES), F32)
        rows = [None] * SUBLANES
        for s in (range(SUBLANES) if d == 0 else range(SUBLANES - 1, -1, -1)):
            rows[s] = c
            c = h_end[s:s + 1, :] + a_end[s:s + 1, :] * c
        st_ref[d:d + 1, cols] = c
        entering.append(jnp.concatenate(rows, axis=0))

    def fix_body(i, carry):
        rows = step_rows(i)
        for sl in range(n_slabs):
            h_f, h_b = (run_scr[d, 1, sl, rows, :]
                        + run_scr[d, 0, sl, rows, :] * entering[chains.index((d, sl))]
                        for d in range(2))
            h_scr[sl, rows, :] = h_f + h_b
        return carry

    lax.fori_loop(0, seg, fix_body, 0, unroll=SCAN_UNROLL)

    for sl in range(n_slabs):
        cols = slice(sl * LANES, (sl + 1) * LANES)
        for s in range(SUBLANES):
            src, dst = slice(s * pitch, s * pitch + seg), slice(s * seg, (s + 1) * seg)
            o_ref[dst, cols] = (h_scr[sl, src, :]
                                * jax.nn.gelu(y_ref[dst, cols].astype(F32))).astype(o_ref.dtype)


def _lru_core(proj, conv_w, conv_b, gate_w, gate_b, lam, j, *, row0, n_seq, seq_len, state,
              n_blk=2):
    width = proj.shape[1] // 2
    tw = n_blk * LRU_BLOCK
    nb = width // tw
    blk0 = row0 // seq_len
    in_specs = [pl.BlockSpec((seq_len, tw), lambda b, c: (blk0 + b, c)),
                pl.BlockSpec((seq_len, tw), lambda b, c: (blk0 + b, nb + c)),
                _lead((j,), (conv_w.shape[1], tw), lambda b, c: (0, c)),
                _lead((j,), (1, tw), lambda b, c: (0, c)),
                _lead((j,), (2, 2, n_blk, LRU_BLOCK, LRU_BLOCK), lambda b, c: (0, 0, c, 0, 0)),
                _lead((j,), (2, 2, tw), lambda b, c: (0, 0, c)),
                _lead((j,), (2, tw), lambda b, c: (0, c))]
    args = [proj, proj, conv_w, conv_b[:, None, :], gate_w, gate_b, lam]
    if state is not None:
        in_specs.append(pl.BlockSpec((None, None, 2, tw), lambda b, c: (b, j, 0, c)))
        args.append(state)
    body = functools.partial(_lru_kernel, seq_len=seq_len, n_blk=n_blk,
                             has_state=state is not None)
    n_slabs = tw // LANES
    scr_rows = SUBLANES * (seq_len // SUBLANES + SEG_PAD)
    return pl.pallas_call(
        body,
        grid=(n_seq, nb),
        in_specs=in_specs,
        out_specs=[pl.BlockSpec((seq_len, tw), lambda b, c: (b, c)),
                   pl.BlockSpec((None, 2, tw), lambda b, c: (b, 0, c))],
        out_shape=[jax.ShapeDtypeStruct((n_seq * seq_len, width), BF16),
                   jax.ShapeDtypeStruct((n_seq, 2, width), F32)],
        scratch_shapes=[pltpu.VMEM((2, 2, n_slabs, scr_rows, LANES), F32),
                        pltpu.VMEM((2, 2, n_slabs, scr_rows, LANES), F32),
                        pltpu.VMEM((n_slabs, scr_rows, LANES), F32)],
        compiler_params=_params(2),
        name="lru_core",
    )(*args)


def _norm_kernel(x_ref, g_ref, o_ref):
    x = x_ref[...]
    o_ref[...] = (x * lax.rsqrt(jnp.mean(x * x, axis=-1, keepdims=True) + EPS)) * g_ref[...]


def _final_norm(x, gain, row0, rows, tr=512):
    d = x.shape[1]
    blk0 = row0 // tr
    return pl.pallas_call(
        _norm_kernel,
        grid=(rows // tr,),
        in_specs=[pl.BlockSpec((tr, d), lambda i: (blk0 + i, 0)),
                  pl.BlockSpec((1, d), lambda i: (0, 0))],
        out_specs=pl.BlockSpec((tr, d), lambda i: (i, 0)),
        out_shape=jax.ShapeDtypeStruct((rows, d), F32),
        compiler_params=_params(1),
        name="final_norm",
    )(x, gain.reshape(1, d))


def kernel(x_prompt, x_sample, state_ret, state_lru, c, c_ctx, norm_g, final_g, w_mod, b_mod,
           ffn_w_in, ffn_w_out, a_w_in, a_v_g, a_w_s, a_b_s, a_w_out, r_w_in, r_decay, r_gn_g,
           r_w_out, l_w_in, l_conv_w, l_conv_b, l_gate_w, l_gate_b, l_lambda, l_w_out):
    batch, seq, d = x_prompt.shape
    dec_batch, dec_seq, _ = x_sample.shape
    depth = w_mod.shape[0]
    m_prompt, m_sample = batch * seq, dec_batch * dec_seq
    assert m_prompt % TM == 0 and dec_seq % TM == 0 and dec_batch + 1 <= MOD_ROWS
    prompt_tiles, tiles_per_sample = m_prompt // TM, dec_seq // TM

    def seg(i):
        return jnp.maximum(i - prompt_tiles, -1) // tiles_per_sample + 1

    x = jnp.concatenate([x_prompt.reshape(m_prompt, d), x_sample.reshape(m_sample, d)], axis=0)
    cond = jnp.concatenate(
        [c_ctx[None, :], c, jnp.zeros((MOD_ROWS - 1 - dec_batch, d), F32)], axis=0)
    mod_all = _modulation(cond, w_mod, b_mod).reshape(depth, MOD_ROWS, N_MOD, d)
    rope = _rope_tables(dec_seq, d // R_HEADS)

    def ffn(x, l, s, mod, more_casts=()):
        casts = [(ffn_w_out, (l, s), CAST_ROWS)] + list(more_casts)
        hid, w_out, *rounded = _project(x, norm_g[l, 2 * s], mod, ffn_w_in, (l, s), seg,
                                        shift_idx=6 * s, scale_idx=6 * s + 1, act="swiglu",
                                        casts=casts)
        x = _out_project(hid, w_out, x, mod, seg, gate_idx=6 * s + 2, coef=0.5)
        return (x, *rounded)

    ret_states, lru_states = [], []
    for l in range(depth):
        kind, j = l % N_MIXERS, l // N_MIXERS
        mod = mod_all[l]
        mixer_w_in, mixer_w_out = ((a_w_in, a_w_out), (r_w_in, r_w_out), (l_w_in, l_w_out))[kind]
        x, w_in = ffn(x, l, 0, mod, [(mixer_w_in, (j,), MIX_CAST_ROWS)])
        proj, w_out = _project(x, norm_g[l, 1], mod, w_in, (), seg, shift_idx=3, scale_idx=4,
                               act="gelu" if kind == 0 else "none",
                               casts=[(mixer_w_out, (j,), CAST_ROWS)])
        if kind == 0:
            mixed = _gmlp_core(proj, a_v_g, a_w_s, a_b_s, j)
        elif kind == 1:
            o_p, st = _retention_core(proj, r_decay, r_gn_g, j, row0=0, n_seq=batch,
                                      seq_len=seq, state=None, rope=None, emit_state=True,
                                      hb=4)
            (o_s,) = _retention_core(proj, r_decay, r_gn_g, j, row0=m_prompt, n_seq=dec_batch,
                                     seq_len=dec_seq, state=state_ret, rope=rope,
                                     emit_state=False)
            mixed = (o_p, o_s)
            ret_states.append(st)
        else:
            o_p, st = _lru_core(proj, l_conv_w, l_conv_b, l_gate_w, l_gate_b, l_lambda, j, row0=0,
                                n_seq=batch, seq_len=seq, state=None)
            o_s, _ = _lru_core(proj, l_conv_w, l_conv_b, l_gate_w, l_gate_b, l_lambda, j,
                               row0=m_prompt, n_seq=dec_batch, seq_len=dec_seq, state=state_lru)
            mixed = (o_p, o_s)
            lru_states.append(st)
        x = _out_project(mixed, w_out, x, mod, seg, gate_idx=5, coef=1.0)
        (x,) = ffn(x, l, 1, mod)

    y_prompt = _final_norm(x, final_g, 0, m_prompt).reshape(batch, seq, d)
    y_sample = _final_norm(x, final_g, m_prompt, m_sample).reshape(dec_batch, dec_seq, d)
    new_state_ret = jnp.stack(ret_states, axis=1)
    new_state_lru = jnp.stack(lru_states, axis=1)
    return (y_prompt, y_sample, new_state_ret, new_state_lru)
```

```python
import functools

import jax
import jax.numpy as jnp
from jax import lax
from jax.experimental import pallas as pl
from jax.experimental.pallas import tpu as pltpu

F32 = jnp.float32
BF16 = jnp.bfloat16

EPS = 1e-6
N_MIXERS = 3
N_MOD = 9
GRID_W = 64
A_CHUNK = 128
A_GROUPS = 8
R_HEADS = 8
R_CHUNK = 256
ROPE_BASE = 10000.0
LRU_BLOCK = 256
LRU_C = 8.0

LANES = 128
SUBLANES = 8
MXU_N = 256
SEG_PAD = 4
SCAN_UNROLL = 8
MOD_ROWS = SUBLANES
TM = 1024
ROW_CHUNK = 16
CAST_ROWS = 256
MIX_CAST_ROWS = 32
VMEM_LIMIT = 56 * 1024 * 1024
OUT_VMEM_BUDGET = 52 * 1024 * 1024
PROJ_VMEM_BUDGET = 52 * 1024 * 1024


def _params(n_grid):
    return pltpu.CompilerParams(dimension_semantics=("arbitrary",) * n_grid,
                                vmem_limit_bytes=VMEM_LIMIT)


def _lead(lead, block, index_map):
    return pl.BlockSpec((None,) * len(lead) + tuple(block),
                        lambda *g: tuple(lead) + tuple(index_map(*g)))


def _log_sigmoid(x):
    return jnp.minimum(x, 0.0) - jnp.log1p(jnp.exp(-jnp.abs(x)))


def _mod_kernel(c_ref, w_ref, b_ref, o_ref):
    s = jax.nn.silu(c_ref[...]).astype(BF16)
    o_ref[...] = jnp.dot(s, w_ref[...].astype(BF16), preferred_element_type=F32) + b_ref[...]


def _modulation(cond, w_mod, b_mod, tn=2048):
    depth, d, n = w_mod.shape
    return pl.pallas_call(
        _mod_kernel,
        grid=(depth, n // tn),
        in_specs=[pl.BlockSpec((MOD_ROWS, d), lambda l, j: (0, 0)),
                  pl.BlockSpec((None, d, tn), lambda l, j: (l, 0, j)),
                  pl.BlockSpec((None, 1, tn), lambda l, j: (l, 0, j))],
        out_specs=pl.BlockSpec((None, MOD_ROWS, tn), lambda l, j: (l, 0, j)),
        out_shape=jax.ShapeDtypeStruct((depth, MOD_ROWS, n), F32),
        compiler_params=_params(2),
        name="modulation",
    )(cond, w_mod, b_mod.reshape(depth, 1, n))


def _round_chunk(src_ref, dst_ref):
    dst_ref[...] = src_ref[...].astype(BF16)


def _proj_kernel(x_ref, g_ref, mod_ref, *refs, n_w, n_steps, cast_steps, shift_idx, scale_idx,
                 act):
    n_c = len(cast_steps)
    w_refs, wc_refs = refs[:n_w], refs[n_w:n_w + n_c]
    o_ref, wb_refs, h_ref = refs[n_w + n_c], refs[n_w + n_c + 1:-1], refs[-1]

    @pl.when(pl.program_id(1) == 0)
    def _prologue():
        gain = g_ref[...]
        scale = 1.0 + mod_ref[scale_idx:scale_idx + 1, :]
        shift = mod_ref[shift_idx:shift_idx + 1, :]

        def rows_body(r, carry):
            rows = pl.ds(pl.multiple_of(r * ROW_CHUNK, ROW_CHUNK), ROW_CHUNK)
            x = x_ref[rows, :]
            y = x * lax.rsqrt(jnp.mean(x * x, axis=-1, keepdims=True) + EPS)
            h_ref[rows, :] = ((y * gain) * scale + shift).astype(h_ref.dtype)
            return carry

        lax.fori_loop(0, x_ref.shape[0] // ROW_CHUNK, rows_body, 0, unroll=8)

    step = pl.program_id(0) * pl.num_programs(1) + pl.program_id(1)
    for wc_ref, wb_ref, n_cast in zip(wc_refs, wb_refs, cast_steps):
        if 4 * n_cast >= 3 * n_steps:
            _round_chunk(wc_ref, wb_ref)
        else:
            pl.when(step < n_cast)(functools.partial(_round_chunk, wc_ref, wb_ref))

    h = h_ref[...]
    for c in range(0, o_ref.shape[1], MXU_N):
        cols = slice(c, c + MXU_N)
        ys = [jnp.dot(h, w[:, cols].astype(BF16), preferred_element_type=F32) for w in w_refs]
        if act == "swiglu":
            out = jax.nn.silu(ys[0]) * ys[1]
        elif act == "gelu":
            out = jax.nn.gelu(ys[0])
        else:
            out = ys[0]
        o_ref[:, cols] = out.astype(o_ref.dtype)


def _proj_tile_n(d, n_out, n_w, w_itemsize, cast_bytes):
    fixed = 2 * TM * d * 4 + TM * d * 2 + cast_bytes
    for k in range(n_out // MXU_N, 0, -1):
        tn = k * MXU_N
        blocks = fixed + 2 * (n_w * d * tn * w_itemsize + TM * tn * 2)
        if n_out % tn == 0 and blocks <= PROJ_VMEM_BUDGET:
            return tn
    raise ValueError("projection tiles do not fit VMEM")


def _project(x, gain, mod, w, lead, seg, *, shift_idx, scale_idx, act, casts=()):
    m, d = x.shape
    n_w = 2 if act == "swiglu" else 1
    n_out = w.shape[-1] // n_w
    cast_bytes = sum(2 * rows * wn.shape[-1] * (wn.dtype.itemsize + 2) for wn, _, rows in casts)
    tn = _proj_tile_n(d, n_out, n_w, w.dtype.itemsize, cast_bytes)
    grid = (m // TM, n_out // tn)
    w_specs = [_lead(lead, (d, tn), lambda i, j: (0, j))]
    if act == "swiglu":
        w_specs.append(_lead(lead, (d, tn), lambda i, j: (0, j + grid[1])))
    in_specs = [pl.BlockSpec((TM, d), lambda i, j: (i, 0)),
                pl.BlockSpec((1, d), lambda i, j: (0, 0)),
                pl.BlockSpec((None, N_MOD, d), lambda i, j: (seg(i), 0, 0))] + w_specs
    args = [x, gain.reshape(1, d), mod] + [w] * n_w
    out_specs = [pl.BlockSpec((TM, tn), lambda i, j: (i, j))]
    out_shape = [jax.ShapeDtypeStruct((m, n_out), BF16)]
    cast_steps = []
    for w_next, lead_next, rows in casts:
        k_next, d_next = w_next.shape[-2:]
        n_cast = k_next // rows
        assert k_next % rows == 0 and n_cast <= grid[0] * grid[1]

        def chunk(i, j, n_cast=n_cast):
            return (jnp.minimum(i * grid[1] + j, n_cast - 1), 0)

        in_specs.append(_lead(lead_next, (rows, d_next), chunk))
        args.append(w_next)
        out_specs.append(pl.BlockSpec((rows, d_next), chunk))
        out_shape.append(jax.ShapeDtypeStruct((k_next, d_next), BF16))
        cast_steps.append(n_cast)
    body = functools.partial(_proj_kernel, n_w=n_w, n_steps=grid[0] * grid[1],
                             cast_steps=tuple(cast_steps), shift_idx=shift_idx,
                             scale_idx=scale_idx, act=act)
    return pl.pallas_call(
        body,
        grid=grid,
        in_specs=in_specs,
        out_specs=out_specs,
        out_shape=out_shape,
        scratch_shapes=[pltpu.VMEM((TM, d), BF16)],
        compiler_params=_params(2),
        name="project_" + act,
    )(*args)


def _out_kernel(*refs, n_a, split, gate_idx, coef):
    a_refs = refs[:n_a]
    w_ref, x_ref, mod_ref, o_ref = refs[n_a:]

    def update(a_ref):
        y = jnp.dot(a_ref[...], w_ref[...], preferred_element_type=F32)
        gate = mod_ref[gate_idx:gate_idx + 1, :]
        if coef != 1.0:
            gate = coef * gate
        o_ref[...] = x_ref[...] + gate * y

    if n_a == 1:
        update(a_refs[0])
    else:
        pl.when(pl.program_id(0) < split)(functools.partial(update, a_refs[0]))
        pl.when(pl.program_id(0) >= split)(functools.partial(update, a_refs[1]))


def _out_tile_n(n_a, k, d):
    for tn in (2048, 1024, 512, 256):
        blocks = 2 * (n_a * TM * k * 2 + k * tn * 2 + 2 * TM * tn * 4) + TM * tn * 4
        if d % tn == 0 and blocks <= OUT_VMEM_BUDGET:
            return tn
    raise ValueError("out_project tiles do not fit VMEM")


def _out_project(a, w, x, mod, seg, *, gate_idx, coef):
    m, d = x.shape
    assert w.dtype == BF16
    tn = _out_tile_n(len(a) if isinstance(a, tuple) else 1, w.shape[0], d)
    if isinstance(a, tuple):
        split = a[0].shape[0] // TM
        k = a[0].shape[1]
        a_specs = [pl.BlockSpec((TM, k), lambda i, j: (jnp.minimum(i, split - 1), 0)),
                   pl.BlockSpec((TM, k), lambda i, j: (jnp.maximum(i - split, 0), 0))]
    else:
        split, k = 0, a.shape[1]
        a_specs = [pl.BlockSpec((TM, k), lambda i, j: (i, 0))]
        a = (a,)
    body = functools.partial(_out_kernel, n_a=len(a), split=split, gate_idx=gate_idx, coef=coef)
    return pl.pallas_call(
        body,
        grid=(m // TM, d // tn),
        in_specs=a_specs + [pl.BlockSpec((k, tn), lambda i, j: (0, j)),
                            pl.BlockSpec((TM, tn), lambda i, j: (i, j)),
                            pl.BlockSpec((None, N_MOD, tn), lambda i, j: (seg(i), 0, j))],
        out_specs=pl.BlockSpec((TM, tn), lambda i, j: (i, j)),
        out_shape=jax.ShapeDtypeStruct((m, d), F32),
        compiler_params=_params(2),
        name="out_project",
    )(*a, w, x, mod)


def _gmlp_kernel(u_ref, v_ref, vg_ref, ws_ref, bs_ref, o_ref):
    width = v_ref.shape[1]
    gw = width // A_GROUPS
    vg = vg_ref[...]
    for c in range(v_ref.shape[0] // A_CHUNK):
        rows = slice(c * A_CHUNK, (c + 1) * A_CHUNK)
        v = v_ref[rows, :].astype(F32)
        vn = v * lax.rsqrt(jnp.mean(v * v, axis=-1, keepdims=True) + EPS)
        vn = (vn * vg).astype(BF16)
        for g in range(A_GROUPS):
            cols = slice(g * gw, (g + 1) * gw)
            sv = jnp.dot(ws_ref[g].astype(BF16), vn[:, cols], preferred_element_type=F32)
            sv = sv + bs_ref[:, g:g + 1]
            o_ref[rows, cols] = (u_ref[rows, cols].astype(F32) * sv).astype(o_ref.dtype)


def _gmlp_core(uv, v_g, w_s, b_s, j, rows_per_step=4 * A_CHUNK):
    m, two_w = uv.shape
    width = two_w // 2
    tr = rows_per_step
    return pl.pallas_call(
        _gmlp_kernel,
        grid=(m // tr,),
        in_specs=[pl.BlockSpec((tr, width), lambda i: (i, 0)),
                  pl.BlockSpec((tr, width), lambda i: (i, 1)),
                  _lead((j,), (1, width), lambda i: (0, 0)),
                  _lead((j,), (A_GROUPS, A_CHUNK, A_CHUNK), lambda i: (0, 0, 0)),
                  _lead((j,), (A_CHUNK, A_GROUPS), lambda i: (0, 0))],
        out_specs=pl.BlockSpec((tr, width), lambda i: (i, 0)),
        out_shape=jax.ShapeDtypeStruct((m, width), BF16),
        compiler_params=_params(1),
        name="gmlp_core",
    )(uv, uv, v_g[:, None, :], w_s, jnp.swapaxes(b_s, 1, 2))


def _ret_kernel(*refs, seq_len, dk, hb, has_state, has_rope, emit_state):
    it = iter(refs)
    q_ref, k_ref, v_ref, g_ref, dec_ref, gn_ref = (next(it) for _ in range(6))
    cos_ref, sin_ref = (next(it), next(it)) if has_rope else (None, None)
    s0_ref = next(it) if has_state else None
    o_ref = next(it)
    st_ref = next(it) if emit_state else None
    q_scr, k_scr, acc_scr, r_scr = (next(it) for _ in range(4))
    chunk = R_CHUNK
    n_chunks = seq_len // chunk
    dv = 2 * dk

    ri = lax.broadcasted_iota(jnp.int32, (chunk, chunk), 0).astype(F32)
    ci = lax.broadcasted_iota(jnp.int32, (chunk, chunk), 1).astype(F32)
    rel = ri - ci

    def head(hh):
        qk_cols = slice(hh * dk, (hh + 1) * dk)
        v_cols = slice(hh * dv, (hh + 1) * dv)
        q = q_ref[:, qk_cols].astype(F32)
        k = k_ref[:, qk_cols].astype(F32)
        if has_rope:
            cos, sin = cos_ref[...], sin_ref[...]

            def rope(t):
                swapped = jnp.concatenate(
                    [pltpu.roll(t[:, s:s + LANES], LANES // 2, 1) for s in range(0, dk, LANES)],
                    axis=1)
                return t * cos + swapped * sin

            q, k = rope(q), rope(k)
        q_scr[:, qk_cols] = q.astype(BF16)
        k_scr[:, qk_cols] = k * (dk ** -0.5)

        consts = []
        for direction in range(2):
            lg = _log_sigmoid(dec_ref[direction, hh])
            if direction == 0:
                dmat = jnp.where(rel >= 0, jnp.exp(jnp.maximum(rel, 0.0) * lg), 0.0)
                zeta = jnp.exp((chunk - 1 - ri) * lg)[:, :1]
                xi = jnp.exp((ri + 1) * lg)[:, :1]
            else:
                dmat = jnp.where(rel <= 0, jnp.exp(jnp.maximum(-rel, 0.0) * lg), 0.0)
                zeta = jnp.exp(ri * lg)[:, :1]
                xi = jnp.exp((chunk - ri) * lg)[:, :1]
            consts.append((dmat, zeta, xi, jnp.exp(chunk * lg)[:, :1]))
            if has_state:
                r_scr[direction, hh] = s0_ref[direction, hh].astype(F32)
            else:
                r_scr[direction, hh] = jnp.zeros((dk, dv), F32)

        def chunk_step(direction, i):
            dmat, zeta, xi, g_chunk = consts[direction]
            rows = slice(i * chunk, (i + 1) * chunk)
            qi = q_scr[rows, qk_cols]
            kf = k_scr[rows, qk_cols]
            vi = v_ref[rows, v_cols]
            s = lax.dot_general(qi, kf.astype(BF16), (((1,), (1,)), ((), ())),
                                preferred_element_type=F32)
            o_inner = jnp.dot((s * dmat).astype(BF16), vi, preferred_element_type=F32)
            o_cross = jnp.dot(qi, r_scr[direction, hh].astype(BF16),
                              preferred_element_type=F32) * xi
            kv = lax.dot_general((kf * zeta).astype(BF16), vi, (((0,), (0,)), ((), ())),
                                 preferred_element_type=F32)
            r_scr[direction, hh] = g_chunk * r_scr[direction, hh] + kv
            acc_scr[direction, rows, v_cols] = o_inner + o_cross

        for step in range(n_chunks):
            chunk_step(0, step)
            chunk_step(1, n_chunks - 1 - step)

        gn = gn_ref[:, v_cols]
        for i in range(n_chunks):
            rows = slice(i * chunk, (i + 1) * chunk)
            o = acc_scr[0, rows, v_cols] + acc_scr[1, rows, v_cols]
            mu = jnp.mean(o, axis=-1, keepdims=True)
            var = jnp.mean(jnp.square(o - mu), axis=-1, keepdims=True)
            on = ((o - mu) * lax.rsqrt(var + EPS)) * gn
            o_ref[rows, v_cols] = (jax.nn.silu(g_ref[rows, v_cols].astype(F32)) * on
                                   ).astype(o_ref.dtype)

    for hh in range(hb):
        head(hh)
    if emit_state:
        st_ref[...] = r_scr[...]


def _retention_core(proj, decay, gn_g, j, *, row0, n_seq, seq_len, state, rope, emit_state,
                    hb=2):
    heads = R_HEADS
    dk = proj.shape[1] // (6 * heads)
    dv = 2 * dk
    blk0 = row0 // seq_len
    hg = heads // hb
    dec = jnp.broadcast_to(decay[j][:, :, None, None], (2, heads, 1, R_CHUNK))
    in_specs = [pl.BlockSpec((seq_len, hb * dk), lambda b, h: (blk0 + b, h)),
                pl.BlockSpec((seq_len, hb * dk), lambda b, h: (blk0 + b, hg + h)),
                pl.BlockSpec((seq_len, hb * dv), lambda b, h: (blk0 + b, hg + h)),
                pl.BlockSpec((seq_len, hb * dv), lambda b, h: (blk0 + b, 2 * hg + h)),
                pl.BlockSpec((2, hb, 1, R_CHUNK), lambda b, h: (0, h, 0, 0)),
                _lead((j,), (1, hb * dv), lambda b, h: (0, h))]
    args = [proj, proj, proj, proj, dec, gn_g[:, None, :]]
    if rope is not None:
        in_specs += [pl.BlockSpec((seq_len, dk), lambda b, h: (0, 0))] * 2
        args += list(rope)
    if state is not None:
        in_specs.append(pl.BlockSpec((None, None, 2, hb, dk, dv),
                                     lambda b, h: (b, j, 0, h, 0, 0)))
        args.append(state)
    out_specs = [pl.BlockSpec((seq_len, hb * dv), lambda b, h: (b, h))]
    out_shape = [jax.ShapeDtypeStruct((n_seq * seq_len, heads * dv), BF16)]
    if emit_state:
        out_specs.append(pl.BlockSpec((None, 2, hb, dk, dv), lambda b, h: (b, 0, h, 0, 0)))
        out_shape.append(jax.ShapeDtypeStruct((n_seq, 2, heads, dk, dv), F32))
    body = functools.partial(_ret_kernel, seq_len=seq_len, dk=dk, hb=hb,
                             has_state=state is not None, has_rope=rope is not None,
                             emit_state=emit_state)
    return pl.pallas_call(
        body,
        grid=(n_seq, hg),
        in_specs=in_specs,
        out_specs=out_specs,
        out_shape=out_shape,
        scratch_shapes=[pltpu.VMEM((seq_len, hb * dk), BF16),
                        pltpu.VMEM((seq_len, hb * dk), F32),
                        pltpu.VMEM((2, seq_len, hb * dv), F32),
                        pltpu.VMEM((2, hb, dk, dv), F32)],
        compiler_params=_params(2),
        name="retention_core",
    )(*args)


def _rope_tables(seq_len, dk):
    nf = dk // 4
    inv = ROPE_BASE ** (-jnp.arange(nf, dtype=F32) / nf)
    t = jnp.arange(seq_len)
    tabs = []
    for p in (t // GRID_W, t % GRID_W):
        ang = p.astype(F32)[:, None] * inv
        tabs.append((jnp.cos(ang), jnp.sin(ang)))
    cos = jnp.concatenate([tabs[0][0], tabs[0][0], tabs[1][0], tabs[1][0]], axis=1)
    sin = jnp.concatenate([-tabs[0][1], tabs[0][1], -tabs[1][1], tabs[1][1]], axis=1)
    return cos, sin


def _lru_kernel(*refs, seq_len, n_blk, has_state):
    it = iter(refs)
    y_ref, x_ref, cw_ref, cb_ref, gw_ref, gb_ref, lam_ref = (next(it) for _ in range(7))
    h0_ref = next(it) if has_state else None
    o_ref, st_ref, ab_scr, run_scr, h_scr = (next(it) for _ in range(5))
    n = seq_len
    row = lax.broadcasted_iota(jnp.int32, (n, LRU_BLOCK), 0)

    def shifted(v, d, fill):
        if d == 0:
            return v
        if d > 0:
            return jnp.where(row >= d, pltpu.roll(v, d, 0), fill)
        return jnp.where(row < n + d, pltpu.roll(v, n + d, 0), fill)

    seg = n // SUBLANES
    pitch = seg + SEG_PAD
    n_slabs = x_ref.shape[1] // LANES
    slabs_per_block = LRU_BLOCK // LANES
    conv_w = cw_ref.shape[0]
    pad_l = conv_w // 2
    for blk in range(n_blk):
        bc = slice(blk * LRU_BLOCK, (blk + 1) * LRU_BLOCK)
        x = x_ref[:, bc].astype(F32)
        xc = shifted(x, pad_l, 0.0) * cw_ref[0:1, bc]
        for t in range(1, conv_w):
            xc = xc + shifted(x, pad_l - t, 0.0) * cw_ref[t:t + 1, bc]
        xc = xc + cb_ref[:, bc]
        xcb = xc.astype(BF16)
        for direction in range(2):
            gates = [jax.nn.sigmoid(
                jnp.dot(xcb, gw_ref[direction, gi, blk].astype(BF16), preferred_element_type=F32)
                + gb_ref[direction, gi:gi + 1, bc]) for gi in range(2)]
            log_a = (LRU_C * gates[0]) * _log_sigmoid(lam_ref[direction:direction + 1, bc])
            a = jnp.exp(log_a)
            b = (jnp.sqrt(-jnp.tanh(log_a) * (a * a + 1.0)) * gates[1]) * xc
            for sl in range(slabs_per_block):
                slab = blk * slabs_per_block + sl
                cols = slice(sl * LANES, (sl + 1) * LANES)
                for s in range(SUBLANES):
                    src, dst = slice(s * seg, (s + 1) * seg), slice(s * pitch, s * pitch + seg)
                    ab_scr[direction, 0, slab, dst, :] = a[src, cols]
                    ab_scr[direction, 1, slab, dst, :] = b[src, cols]

    def step_rows(i):
        return pl.ds(i, SUBLANES, stride=pitch)

    chains = [(d, sl) for d in range(2) for sl in range(n_slabs)]

    def scan_body(i, carry):
        out = []
        for (d, sl), (a_run, h_run) in zip(chains, carry):
            rows = step_rows(i if d == 0 else seg - 1 - i)
            a_i = ab_scr[d, 0, sl, rows, :]
            h_run = a_i * h_run + ab_scr[d, 1, sl, rows, :]
            a_run = a_i * a_run
            run_scr[d, 0, sl, rows, :] = a_run
            run_scr[d, 1, sl, rows, :] = h_run
            out.append((a_run, h_run))
        return tuple(out)

    init = tuple((jnp.ones((SUBLANES, LANES), F32), jnp.zeros((SUBLANES, LANES), F32))
                 for _ in chains)
    ends = lax.fori_loop(0, seg, scan_body, init, unroll=SCAN_UNROLL)

    entering = []
    for (d, sl), (a_end, h_end) in zip(chains, ends):
        cols = slice(sl * LANES, (sl + 1) * LANES)
        c = h0_ref[d:d + 1, cols] if has_state else jnp.zeros((1, LANES), F32)
        rows = [None] * SUBLANES
        for s in (range(SUBLANES) if d == 0 else range(SUBLANES - 1, -1, -1)):
            rows[s] = c
            c = h_end[s:s + 1, :] + a_end[s:s + 1, :] * c
        st_ref[d:d + 1, cols] = c
        entering.append(jnp.concatenate(rows, axis=0))

    def fix_body(i, carry):
        rows = step_rows(i)
        for sl in range(n_slabs):
            h_f, h_b = (run_scr[d, 1, sl, rows, :]
                        + run_scr[d, 0, sl, rows, :] * entering[chains.index((d, sl))]
                        for d in range(2))
            h_scr[sl, rows, :] = h_f + h_b
        return carry

    lax.fori_loop(0, seg, fix_body, 0, unroll=SCAN_UNROLL)

    for sl in range(n_slabs):
        cols = slice(sl * LANES, (sl + 1) * LANES)
        for s in range(SUBLANES):
            src, dst = slice(s * pitch, s * pitch + seg), slice(s * seg, (s + 1) * seg)
            o_ref[dst, cols] = (h_scr[sl, src, :]
                                * jax.nn.gelu(y_ref[dst, cols].astype(F32))).astype(o_ref.dtype)


def _lru_core(proj, conv_w, conv_b, gate_w, gate_b, lam, j, *, row0, n_seq, seq_len, state,
              n_blk=2):
    width = proj.shape[1] // 2
    tw = n_blk * LRU_BLOCK
    nb = width // tw
    blk0 = row0 // seq_len
    in_specs = [pl.BlockSpec((seq_len, tw), lambda b, c: (blk0 + b, c)),
                pl.BlockSpec((seq_len, tw), lambda b, c: (blk0 + b, nb + c)),
                _lead((j,), (conv_w.shape[1], tw), lambda b, c: (0, c)),
                _lead((j,), (1, tw), lambda b, c: (0, c)),
                _lead((j,), (2, 2, n_blk, LRU_BLOCK, LRU_BLOCK), lambda b, c: (0, 0, c, 0, 0)),
                _lead((j,), (2, 2, tw), lambda b, c: (0, 0, c)),
                _lead((j,), (2, tw), lambda b, c: (0, c))]
    args = [proj, proj, conv_w, conv_b[:, None, :], gate_w, gate_b, lam]
    if state is not None:
        in_specs.append(pl.BlockSpec((None, None, 2, tw), lambda b, c: (b, j, 0, c)))
        args.append(state)
    body = functools.partial(_lru_kernel, seq_len=seq_len, n_blk=n_blk,
                             has_state=state is not None)
    n_slabs = tw // LANES
    scr_rows = SUBLANES * (seq_len // SUBLANES + SEG_PAD)
    return pl.pallas_call(
        body,
        grid=(n_seq, nb),
        in_specs=in_specs,
        out_specs=[pl.BlockSpec((seq_len, tw), lambda b, c: (b, c)),
                   pl.BlockSpec((None, 2, tw), lambda b, c: (b, 0, c))],
        out_shape=[jax.ShapeDtypeStruct((n_seq * seq_len, width), BF16),
                   jax.ShapeDtypeStruct((n_seq, 2, width), F32)],
        scratch_shapes=[pltpu.VMEM((2, 2, n_slabs, scr_rows, LANES), F32),
                        pltpu.VMEM((2, 2, n_slabs, scr_rows, LANES), F32),
                        pltpu.VMEM((n_slabs, scr_rows, LANES), F32)],
        compiler_params=_params(2),
        name="lru_core",
    )(*args)


def _norm_kernel(x_ref, g_ref, o_ref):
    x = x_ref[...]
    o_ref[...] = (x * lax.rsqrt(jnp.mean(x * x, axis=-1, keepdims=True) + EPS)) * g_ref[...]


def _final_norm(x, gain, row0, rows, tr=512):
    d = x.shape[1]
    blk0 = row0 // tr
    return pl.pallas_call(
        _norm_kernel,
        grid=(rows // tr,),
        in_specs=[pl.BlockSpec((tr, d), lambda i: (blk0 + i, 0)),
                  pl.BlockSpec((1, d), lambda i: (0, 0))],
        out_specs=pl.BlockSpec((tr, d), lambda i: (i, 0)),
        out_shape=jax.ShapeDtypeStruct((rows, d), F32),
        compiler_params=_params(1),
        name="final_norm",
    )(x, gain.reshape(1, d))


def kernel(x_prompt, x_sample, state_ret, state_lru, c, c_ctx, norm_g, final_g, w_mod, b_mod,
           ffn_w_in, ffn_w_out, a_w_in, a_v_g, a_w_s, a_b_s, a_w_out, r_w_in, r_decay, r_gn_g,
           r_w_out, l_w_in, l_conv_w, l_conv_b, l_gate_w, l_gate_b, l_lambda, l_w_out):
    batch, seq, d = x_prompt.shape
    dec_batch, dec_seq, _ = x_sample.shape
    depth = w_mod.shape[0]
    m_prompt, m_sample = batch * seq, dec_batch * dec_seq
    assert m_prompt % TM == 0 and dec_seq % TM == 0 and dec_batch + 1 <= MOD_ROWS
    prompt_tiles, tiles_per_sample = m_prompt // TM, dec_seq // TM

    def seg(i):
        return jnp.maximum(i - prompt_tiles, -1) // tiles_per_sample + 1

    x = jnp.concatenate([x_prompt.reshape(m_prompt, d), x_sample.reshape(m_sample, d)], axis=0)
    cond = jnp.concatenate(
        [c_ctx[None, :], c, jnp.zeros((MOD_ROWS - 1 - dec_batch, d), F32)], axis=0)
    mod_all = _modulation(cond, w_mod, b_mod).reshape(depth, MOD_ROWS, N_MOD, d)
    rope = _rope_tables(dec_seq, d // R_HEADS)

    def ffn(x, l, s, mod, more_casts=()):
        casts = [(ffn_w_out, (l, s), CAST_ROWS)] + list(more_casts)
        hid, w_out, *rounded = _project(x, norm_g[l, 2 * s], mod, ffn_w_in, (l, s), seg,
                                        shift_idx=6 * s, scale_idx=6 * s + 1, act="swiglu",
                                        casts=casts)
        x = _out_project(hid, w_out, x, mod, seg, gate_idx=6 * s + 2, coef=0.5)
        return (x, *rounded)

    ret_states, lru_states = [], []
    for l in range(depth):
        kind, j = l % N_MIXERS, l // N_MIXERS
        mod = mod_all[l]
        mixer_w_in, mixer_w_out = ((a_w_in, a_w_out), (r_w_in, r_w_out), (l_w_in, l_w_out))[kind]
        x, w_in = ffn(x, l, 0, mod, [(mixer_w_in, (j,), MIX_CAST_ROWS)])
        proj, w_out = _project(x, norm_g[l, 1], mod, w_in, (), seg, shift_idx=3, scale_idx=4,
                               act="gelu" if kind == 0 else "none",
                               casts=[(mixer_w_out, (j,), CAST_ROWS)])
        if kind == 0:
            mixed = _gmlp_core(proj, a_v_g, a_w_s, a_b_s, j)
        elif kind == 1:
            o_p, st = _retention_core(proj, r_decay, r_gn_g, j, row0=0, n_seq=batch,
                                      seq_len=seq, state=None, rope=None, emit_state=True,
                                      hb=8)
            (o_s,) = _retention_core(proj, r_decay, r_gn_g, j, row0=m_prompt, n_seq=dec_batch,
                                     seq_len=dec_seq, state=state_ret, rope=rope,
                                     emit_state=False)
            mixed = (o_p, o_s)
            ret_states.append(st)
        else:
            o_p, st = _lru_core(proj, l_conv_w, l_conv_b, l_gate_w, l_gate_b, l_lambda, j, row0=0,
                                n_seq=batch, seq_len=seq, state=None, n_blk=5)
            o_s, _ = _lru_core(proj, l_conv_w, l_conv_b, l_gate_w, l_gate_b, l_lambda, j,
                               row0=m_prompt, n_seq=dec_batch, seq_len=dec_seq, state=state_lru)
            mixed = (o_p, o_s)
            lru_states.append(st)
        x = _out_project(mixed, w_out, x, mod, seg, gate_idx=5, coef=1.0)
        (x,) = ffn(x, l, 1, mod)

    y_prompt = _final_norm(x, final_g, 0, m_prompt).reshape(batch, seq, d)
    y_sample = _final_norm(x, final_g, m_prompt, m_sample).reshape(dec_batch, dec_seq, d)
    new_state_ret = jnp.stack(ret_states, axis=1)
    new_state_lru = jnp.stack(lru_states, axis=1)
    return (y_prompt, y_sample, new_state_ret, new_state_lru)
```

```python
import functools

import jax
import jax.numpy as jnp
from jax import lax
from jax.experimental import pallas as pl
from jax.experimental.pallas import tpu as pltpu

F32 = jnp.float32
BF16 = jnp.bfloat16

EPS = 1e-6
N_MIXERS = 3
N_MOD = 9
GRID_W = 64
A_CHUNK = 128
A_GROUPS = 8
R_HEADS = 8
R_CHUNK = 256
ROPE_BASE = 10000.0
LRU_BLOCK = 256
LRU_C = 8.0

LANES = 128
SUBLANES = 8
MXU_N = 256
SEG_PAD = 4
SCAN_UNROLL = 8
MOD_ROWS = SUBLANES
TM = 1024
ROW_CHUNK = 16
CAST_ROWS = 256
MIX_CAST_ROWS = 32
VMEM_LIMIT = 56 * 1024 * 1024
OUT_VMEM_BUDGET = 52 * 1024 * 1024
PROJ_VMEM_BUDGET = 52 * 1024 * 1024


def _params(n_grid):
    return pltpu.CompilerParams(dimension_semantics=("arbitrary",) * n_grid,
                                vmem_limit_bytes=VMEM_LIMIT)


def _lead(lead, block, index_map):
    return pl.BlockSpec((None,) * len(lead) + tuple(block),
                        lambda *g: tuple(lead) + tuple(index_map(*g)))


def _log_sigmoid(x):
    return jnp.minimum(x, 0.0) - jnp.log1p(jnp.exp(-jnp.abs(x)))


def _mod_kernel(c_ref, w_ref, b_ref, o_ref):
    s = jax.nn.silu(c_ref[...]).astype(BF16)
    o_ref[...] = jnp.dot(s, w_ref[...].astype(BF16), preferred_element_type=F32) + b_ref[...]


def _modulation(cond, w_mod, b_mod, tn=2048):
    depth, d, n = w_mod.shape
    return pl.pallas_call(
        _mod_kernel,
        grid=(depth, n // tn),
        in_specs=[pl.BlockSpec((MOD_ROWS, d), lambda l, j: (0, 0)),
                  pl.BlockSpec((None, d, tn), lambda l, j: (l, 0, j)),
                  pl.BlockSpec((None, 1, tn), lambda l, j: (l, 0, j))],
        out_specs=pl.BlockSpec((None, MOD_ROWS, tn), lambda l, j: (l, 0, j)),
        out_shape=jax.ShapeDtypeStruct((depth, MOD_ROWS, n), F32),
        compiler_params=_params(2),
        name="modulation",
    )(cond, w_mod, b_mod.reshape(depth, 1, n))


def _round_chunk(src_ref, dst_ref):
    dst_ref[...] = src_ref[...].astype(BF16)


def _proj_kernel(x_ref, g_ref, mod_ref, *refs, n_w, cast_steps, shift_idx, scale_idx, act):
    n_c = len(cast_steps)
    w_refs, wc_refs = refs[:n_w], refs[n_w:n_w + n_c]
    o_ref, wb_refs, h_ref = refs[n_w + n_c], refs[n_w + n_c + 1:-1], refs[-1]

    @pl.when(pl.program_id(1) == 0)
    def _prologue():
        gain = g_ref[...]
        scale = 1.0 + mod_ref[scale_idx:scale_idx + 1, :]
        shift = mod_ref[shift_idx:shift_idx + 1, :]

        def rows_body(r, carry):
            rows = pl.ds(pl.multiple_of(r * ROW_CHUNK, ROW_CHUNK), ROW_CHUNK)
            x = x_ref[rows, :]
            y = x * lax.rsqrt(jnp.mean(x * x, axis=-1, keepdims=True) + EPS)
            h_ref[rows, :] = ((y * gain) * scale + shift).astype(h_ref.dtype)
            return carry

        lax.fori_loop(0, x_ref.shape[0] // ROW_CHUNK, rows_body, 0, unroll=8)

    step = pl.program_id(0) * pl.num_programs(1) + pl.program_id(1)
    for wc_ref, wb_ref, n_cast in zip(wc_refs, wb_refs, cast_steps):
        pl.when(step < n_cast)(functools.partial(_round_chunk, wc_ref, wb_ref))

    h = h_ref[...]
    for c in range(0, o_ref.shape[1], MXU_N):
        cols = slice(c, c + MXU_N)
        ys = [jnp.dot(h, w[:, cols].astype(BF16), preferred_element_type=F32) for w in w_refs]
        if act == "swiglu":
            out = jax.nn.silu(ys[0]) * ys[1]
        elif act == "gelu":
            out = jax.nn.gelu(ys[0].astype(BF16))
        else:
            out = ys[0]
        o_ref[:, cols] = out.astype(o_ref.dtype)


def _proj_tile_n(d, n_out, n_w, w_itemsize, cast_bytes):
    fixed = 2 * TM * d * 4 + TM * d * 2 + cast_bytes
    for k in range(n_out // MXU_N, 0, -1):
        tn = k * MXU_N
        blocks = fixed + 2 * (n_w * d * tn * w_itemsize + TM * tn * 2)
        if n_out % tn == 0 and blocks <= PROJ_VMEM_BUDGET:
            return tn
    raise ValueError("projection tiles do not fit VMEM")


def _project(x, gain, mod, w, lead, seg, *, shift_idx, scale_idx, act, casts=()):
    m, d = x.shape
    n_w = 2 if act == "swiglu" else 1
    n_out = w.shape[-1] // n_w
    cast_bytes = sum(2 * rows * wn.shape[-1] * (wn.dtype.itemsize + 2) for wn, _, rows in casts)
    tn = _proj_tile_n(d, n_out, n_w, w.dtype.itemsize, cast_bytes)
    grid = (m // TM, n_out // tn)
    w_specs = [_lead(lead, (d, tn), lambda i, j: (0, j))]
    if act == "swiglu":
        w_specs.append(_lead(lead, (d, tn), lambda i, j: (0, j + grid[1])))
    in_specs = [pl.BlockSpec((TM, d), lambda i, j: (i, 0)),
                pl.BlockSpec((1, d), lambda i, j: (0, 0)),
                pl.BlockSpec((None, N_MOD, d), lambda i, j: (seg(i), 0, 0))] + w_specs
    args = [x, gain.reshape(1, d), mod] + [w] * n_w
    out_specs = [pl.BlockSpec((TM, tn), lambda i, j: (i, j))]
    out_shape = [jax.ShapeDtypeStruct((m, n_out), BF16)]
    cast_steps = []
    for w_next, lead_next, rows in casts:
        k_next, d_next = w_next.shape[-2:]
        n_cast = k_next // rows
        assert k_next % rows == 0 and n_cast <= grid[0] * grid[1]

        def chunk(i, j, n_cast=n_cast):
            return (jnp.minimum(i * grid[1] + j, n_cast - 1), 0)

        in_specs.append(_lead(lead_next, (rows, d_next), chunk))
        args.append(w_next)
        out_specs.append(pl.BlockSpec((rows, d_next), chunk))
        out_shape.append(jax.ShapeDtypeStruct((k_next, d_next), BF16))
        cast_steps.append(n_cast)
    body = functools.partial(_proj_kernel, n_w=n_w, cast_steps=tuple(cast_steps),
                             shift_idx=shift_idx, scale_idx=scale_idx, act=act)
    return pl.pallas_call(
        body,
        grid=grid,
        in_specs=in_specs,
        out_specs=out_specs,
        out_shape=out_shape,
        scratch_shapes=[pltpu.VMEM((TM, d), BF16)],
        compiler_params=_params(2),
        name="project_" + act,
    )(*args)


def _out_kernel(*refs, n_a, split, gate_idx, coef):
    a_refs = refs[:n_a]
    w_ref, x_ref, mod_ref, o_ref = refs[n_a:]

    def update(a_ref):
        y = jnp.dot(a_ref[...], w_ref[...], preferred_element_type=F32)
        gate = mod_ref[gate_idx:gate_idx + 1, :]
        if coef != 1.0:
            gate = coef * gate
        o_ref[...] = x_ref[...] + gate * y

    if n_a == 1:
        update(a_refs[0])
    else:
        pl.when(pl.program_id(0) < split)(functools.partial(update, a_refs[0]))
        pl.when(pl.program_id(0) >= split)(functools.partial(update, a_refs[1]))


def _out_tile_n(n_a, k, d):
    for tn in (2048, 1024, 512, 256):
        blocks = 2 * (n_a * TM * k * 2 + k * tn * 2 + 2 * TM * tn * 4) + TM * tn * 4
        if d % tn == 0 and blocks <= OUT_VMEM_BUDGET:
            return tn
    raise ValueError("out_project tiles do not fit VMEM")


def _out_project(a, w, x, mod, seg, *, gate_idx, coef):
    m, d = x.shape
    assert w.dtype == BF16
    tn = _out_tile_n(len(a) if isinstance(a, tuple) else 1, w.shape[0], d)
    if isinstance(a, tuple):
        split = a[0].shape[0] // TM
        k = a[0].shape[1]
        a_specs = [pl.BlockSpec((TM, k), lambda i, j: (jnp.minimum(i, split - 1), 0)),
                   pl.BlockSpec((TM, k), lambda i, j: (jnp.maximum(i - split, 0), 0))]
    else:
        split, k = 0, a.shape[1]
        a_specs = [pl.BlockSpec((TM, k), lambda i, j: (i, 0))]
        a = (a,)
    body = functools.partial(_out_kernel, n_a=len(a), split=split, gate_idx=gate_idx, coef=coef)
    return pl.pallas_call(
        body,
        grid=(m // TM, d // tn),
        in_specs=a_specs + [pl.BlockSpec((k, tn), lambda i, j: (0, j)),
                            pl.BlockSpec((TM, tn), lambda i, j: (i, j)),
                            pl.BlockSpec((None, N_MOD, tn), lambda i, j: (seg(i), 0, j))],
        out_specs=pl.BlockSpec((TM, tn), lambda i, j: (i, j)),
        out_shape=jax.ShapeDtypeStruct((m, d), F32),
        compiler_params=_params(2),
        name="out_project",
    )(*a, w, x, mod)


def _gmlp_kernel(u_ref, v_ref, vg_ref, ws_ref, bs_ref, o_ref):
    width = v_ref.shape[1]
    gw = width // A_GROUPS
    vg = vg_ref[...]
    for c in range(v_ref.shape[0] // A_CHUNK):
        rows = slice(c * A_CHUNK, (c + 1) * A_CHUNK)
        v = v_ref[rows, :].astype(F32)
        vn = v * lax.rsqrt(jnp.mean(v * v, axis=-1, keepdims=True) + EPS)
        vn = (vn * vg).astype(BF16)
        for g in range(A_GROUPS):
            cols = slice(g * gw, (g + 1) * gw)
            sv = jnp.dot(ws_ref[g].astype(BF16), vn[:, cols], preferred_element_type=F32)
            sv = sv + bs_ref[:, g:g + 1]
            o_ref[rows, cols] = (u_ref[rows, cols].astype(F32) * sv).astype(o_ref.dtype)


def _gmlp_core(uv, v_g, w_s, b_s, j, rows_per_step=4 * A_CHUNK):
    m, two_w = uv.shape
    width = two_w // 2
    tr = rows_per_step
    return pl.pallas_call(
        _gmlp_kernel,
        grid=(m // tr,),
        in_specs=[pl.BlockSpec((tr, width), lambda i: (i, 0)),
                  pl.BlockSpec((tr, width), lambda i: (i, 1)),
                  _lead((j,), (1, width), lambda i: (0, 0)),
                  _lead((j,), (A_GROUPS, A_CHUNK, A_CHUNK), lambda i: (0, 0, 0)),
                  _lead((j,), (A_CHUNK, A_GROUPS), lambda i: (0, 0))],
        out_specs=pl.BlockSpec((tr, width), lambda i: (i, 0)),
        out_shape=jax.ShapeDtypeStruct((m, width), BF16),
        compiler_params=_params(1),
        name="gmlp_core",
    )(uv, uv, v_g[:, None, :], w_s, jnp.swapaxes(b_s, 1, 2))


def _ret_kernel(*refs, seq_len, dk, hb, has_state, has_rope, emit_state):
    it = iter(refs)
    q_ref, k_ref, v_ref, g_ref, dec_ref, gn_ref = (next(it) for _ in range(6))
    cos_ref, sin_ref = (next(it), next(it)) if has_rope else (None, None)
    s0_ref = next(it) if has_state else None
    o_ref = next(it)
    st_ref = next(it) if emit_state else None
    q_scr, k_scr, acc_scr, r_scr = (next(it) for _ in range(4))
    chunk = R_CHUNK
    n_chunks = seq_len // chunk
    dv = 2 * dk

    ri = lax.broadcasted_iota(jnp.int32, (chunk, chunk), 0).astype(F32)
    ci = lax.broadcasted_iota(jnp.int32, (chunk, chunk), 1).astype(F32)
    rel = ri - ci

    def head(hh):
        qk_cols = slice(hh * dk, (hh + 1) * dk)
        v_cols = slice(hh * dv, (hh + 1) * dv)
        q = q_ref[:, qk_cols].astype(F32)
        k = k_ref[:, qk_cols].astype(F32)
        if has_rope:
            cos, sin = cos_ref[...], sin_ref[...]

            def rope(t):
                swapped = jnp.concatenate(
                    [pltpu.roll(t[:, s:s + LANES], LANES // 2, 1) for s in range(0, dk, LANES)],
                    axis=1)
                return t * cos + swapped * sin

            q, k = rope(q), rope(k)
        q_scr[:, qk_cols] = q.astype(BF16)
        k_scr[:, qk_cols] = k * (dk ** -0.5)

        consts = []
        for direction in range(2):
            lg = _log_sigmoid(dec_ref[direction, hh])
            if direction == 0:
                dmat = jnp.where(rel >= 0, jnp.exp(jnp.maximum(rel, 0.0) * lg), 0.0)
                zeta = jnp.exp((chunk - 1 - ri) * lg)[:, :1]
                xi = jnp.exp((ri + 1) * lg)[:, :1]
            else:
                dmat = jnp.where(rel <= 0, jnp.exp(jnp.maximum(-rel, 0.0) * lg), 0.0)
                zeta = jnp.exp(ri * lg)[:, :1]
                xi = jnp.exp((chunk - ri) * lg)[:, :1]
            consts.append((dmat, zeta, xi, jnp.exp(chunk * lg)[:, :1]))
            if has_state:
                r_scr[direction, hh] = s0_ref[direction, hh].astype(F32)
            else:
                r_scr[direction, hh] = jnp.zeros((dk, dv), F32)

        def chunk_step(direction, i):
            dmat, zeta, xi, g_chunk = consts[direction]
            rows = slice(i * chunk, (i + 1) * chunk)
            qi = q_scr[rows, qk_cols]
            kf = k_scr[rows, qk_cols]
            vi = v_ref[rows, v_cols]
            s = lax.dot_general(qi, kf.astype(BF16), (((1,), (1,)), ((), ())),
                                preferred_element_type=F32)
            o_inner = jnp.dot((s * dmat).astype(BF16), vi, preferred_element_type=F32)
            o_cross = jnp.dot(qi, r_scr[direction, hh].astype(BF16),
                              preferred_element_type=F32) * xi
            kv = lax.dot_general((kf * zeta).astype(BF16), vi, (((0,), (0,)), ((), ())),
                                 preferred_element_type=F32)
            r_scr[direction, hh] = g_chunk * r_scr[direction, hh] + kv
            acc_scr[direction, rows, v_cols] = o_inner + o_cross

        for step in range(n_chunks):
            chunk_step(0, step)
            chunk_step(1, n_chunks - 1 - step)

        gn = gn_ref[:, v_cols]
        for i in range(n_chunks):
            rows = slice(i * chunk, (i + 1) * chunk)
            o = acc_scr[0, rows, v_cols] + acc_scr[1, rows, v_cols]
            mu = jnp.mean(o, axis=-1, keepdims=True)
            var = jnp.mean(jnp.square(o - mu), axis=-1, keepdims=True)
            on = ((o - mu) * lax.rsqrt(var + EPS)) * gn
            o_ref[rows, v_cols] = (jax.nn.silu(g_ref[rows, v_cols].astype(F32)) * on
                                   ).astype(o_ref.dtype)

    for hh in range(hb):
        head(hh)
    if emit_state:
        st_ref[...] = r_scr[...]


def _retention_core(proj, decay, gn_g, j, *, row0, n_seq, seq_len, state, rope, emit_state,
                    hb=2):
    heads = R_HEADS
    dk = proj.shape[1] // (6 * heads)
    dv = 2 * dk
    blk0 = row0 // seq_len
    hg = heads // hb
    dec = jnp.broadcast_to(decay[j][:, :, None, None], (2, heads, 1, R_CHUNK))
    in_specs = [pl.BlockSpec((seq_len, hb * dk), lambda b, h: (blk0 + b, h)),
                pl.BlockSpec((seq_len, hb * dk), lambda b, h: (blk0 + b, hg + h)),
                pl.BlockSpec((seq_len, hb * dv), lambda b, h: (blk0 + b, hg + h)),
                pl.BlockSpec((seq_len, hb * dv), lambda b, h: (blk0 + b, 2 * hg + h)),
                pl.BlockSpec((2, hb, 1, R_CHUNK), lambda b, h: (0, h, 0, 0)),
                _lead((j,), (1, hb * dv), lambda b, h: (0, h))]
    args = [proj, proj, proj, proj, dec, gn_g[:, None, :]]
    if rope is not None:
        in_specs += [pl.BlockSpec((seq_len, dk), lambda b, h: (0, 0))] * 2
        args += list(rope)
    if state is not None:
        in_specs.append(pl.BlockSpec((None, None, 2, hb, dk, dv),
                                     lambda b, h: (b, j, 0, h, 0, 0)))
        args.append(state)
    out_specs = [pl.BlockSpec((seq_len, hb * dv), lambda b, h: (b, h))]
    out_shape = [jax.ShapeDtypeStruct((n_seq * seq_len, heads * dv), BF16)]
    if emit_state:
        out_specs.append(pl.BlockSpec((None, 2, hb, dk, dv), lambda b, h: (b, 0, h, 0, 0)))
        out_shape.append(jax.ShapeDtypeStruct((n_seq, 2, heads, dk, dv), F32))
    body = functools.partial(_ret_kernel, seq_len=seq_len, dk=dk, hb=hb,
                             has_state=state is not None, has_rope=rope is not None,
                             emit_state=emit_state)
    return pl.pallas_call(
        body,
        grid=(n_seq, hg),
        in_specs=in_specs,
        out_specs=out_specs,
        out_shape=out_shape,
        scratch_shapes=[pltpu.VMEM((seq_len, hb * dk), BF16),
                        pltpu.VMEM((seq_len, hb * dk), F32),
                        pltpu.VMEM((2, seq_len, hb * dv), F32),
                        pltpu.VMEM((2, hb, dk, dv), F32)],
        compiler_params=_params(2),
        name="retention_core",
    )(*args)


def _rope_tables(seq_len, dk):
    nf = dk // 4
    inv = ROPE_BASE ** (-jnp.arange(nf, dtype=F32) / nf)
    t = jnp.arange(seq_len)
    tabs = []
    for p in (t // GRID_W, t % GRID_W):
        ang = p.astype(F32)[:, None] * inv
        tabs.append((jnp.cos(ang), jnp.sin(ang)))
    cos = jnp.concatenate([tabs[0][0], tabs[0][0], tabs[1][0], tabs[1][0]], axis=1)
    sin = jnp.concatenate([-tabs[0][1], tabs[0][1], -tabs[1][1], tabs[1][1]], axis=1)
    return cos, sin


def _lru_kernel(*refs, seq_len, n_blk, has_state):
    it = iter(refs)
    y_ref, x_ref, cw_ref, cb_ref, gw_ref, gb_ref, lam_ref = (next(it) for _ in range(7))
    h0_ref = next(it) if has_state else None
    o_ref, st_ref, ab_scr, run_scr, h_scr = (next(it) for _ in range(5))
    n = seq_len
    row = lax.broadcasted_iota(jnp.int32, (n, LRU_BLOCK), 0)

    def shifted(v, d, fill):
        if d == 0:
            return v
        if d > 0:
            return jnp.where(row >= d, pltpu.roll(v, d, 0), fill)
        return jnp.where(row < n + d, pltpu.roll(v, n + d, 0), fill)

    seg = n // SUBLANES
    pitch = seg + SEG_PAD
    n_slabs = x_ref.shape[1] // LANES
    slabs_per_block = LRU_BLOCK // LANES
    conv_w = cw_ref.shape[0]
    pad_l = conv_w // 2
    for blk in range(n_blk):
        bc = slice(blk * LRU_BLOCK, (blk + 1) * LRU_BLOCK)
        x = x_ref[:, bc].astype(F32)
        xc = shifted(x, pad_l, 0.0) * cw_ref[0:1, bc]
        for t in range(1, conv_w):
            xc = xc + shifted(x, pad_l - t, 0.0) * cw_ref[t:t + 1, bc]
        xc = xc + cb_ref[:, bc]
        xcb = xc.astype(BF16)
        for direction in range(2):
            gates = [jax.nn.sigmoid(
                jnp.dot(xcb, gw_ref[direction, gi, blk].astype(BF16), preferred_element_type=F32)
                + gb_ref[direction, gi:gi + 1, bc]) for gi in range(2)]
            log_a = (LRU_C * gates[0]) * _log_sigmoid(lam_ref[direction:direction + 1, bc])
            a = jnp.exp(log_a)
            b = (jnp.sqrt(-jnp.tanh(log_a) * (a * a + 1.0)) * gates[1]) * xc
            for sl in range(slabs_per_block):
                slab = blk * slabs_per_block + sl
                cols = slice(sl * LANES, (sl + 1) * LANES)
                for s in range(SUBLANES):
                    src, dst = slice(s * seg, (s + 1) * seg), slice(s * pitch, s * pitch + seg)
                    ab_scr[direction, 0, slab, dst, :] = a[src, cols]
                    ab_scr[direction, 1, slab, dst, :] = b[src, cols]

    def step_rows(i):
        return pl.ds(i, SUBLANES, stride=pitch)

    chains = [(d, sl) for d in range(2) for sl in range(n_slabs)]

    def scan_body(i, carry):
        out = []
        for (d, sl), (a_run, h_run) in zip(chains, carry):
            rows = step_rows(i if d == 0 else seg - 1 - i)
            a_i = ab_scr[d, 0, sl, rows, :]
            h_run = a_i * h_run + ab_scr[d, 1, sl, rows, :]
            a_run = a_i * a_run
            run_scr[d, 0, sl, rows, :] = a_run
            run_scr[d, 1, sl, rows, :] = h_run
            out.append((a_run, h_run))
        return tuple(out)

    init = tuple((jnp.ones((SUBLANES, LANES), F32), jnp.zeros((SUBLANES, LANES), F32))
                 for _ in chains)
    ends = lax.fori_loop(0, seg, scan_body, init, unroll=SCAN_UNROLL)

    entering = []
    for (d, sl), (a_end, h_end) in zip(chains, ends):
        cols = slice(sl * LANES, (sl + 1) * LANES)
        c = h0_ref[d:d + 1, cols] if has_state else jnp.zeros((1, LANES), F32)
        rows = [None] * SUBLANES
        for s in (range(SUBLANES) if d == 0 else range(SUBLANES - 1, -1, -1)):
            rows[s] = c
            c = h_end[s:s + 1, :] + a_end[s:s + 1, :] * c
        st_ref[d:d + 1, cols] = c
        entering.append(jnp.concatenate(rows, axis=0))

    def fix_body(i, carry):
        rows = step_rows(i)
        for sl in range(n_slabs):
            h_f, h_b = (run_scr[d, 1, sl, rows, :]
                        + run_scr[d, 0, sl, rows, :] * entering[chains.index((d, sl))]
                        for d in range(2))
            h_scr[sl, rows, :] = h_f + h_b
        return carry

    lax.fori_loop(0, seg, fix_body, 0, unroll=SCAN_UNROLL)

    for sl in range(n_slabs):
        cols = slice(sl * LANES, (sl + 1) * LANES)
        for s in range(SUBLANES):
            src, dst = slice(s * pitch, s * pitch + seg), slice(s * seg, (s + 1) * seg)
            o_ref[dst, cols] = (h_scr[sl, src, :]
                                * jax.nn.gelu(y_ref[dst, cols].astype(F32))).astype(o_ref.dtype)


def _lru_core(proj, conv_w, conv_b, gate_w, gate_b, lam, j, *, row0, n_seq, seq_len, state,
              n_blk=2):
    width = proj.shape[1] // 2
    tw = n_blk * LRU_BLOCK
    nb = width // tw
    blk0 = row0 // seq_len
    in_specs = [pl.BlockSpec((seq_len, tw), lambda b, c: (blk0 + b, c)),
                pl.BlockSpec((seq_len, tw), lambda b, c: (blk0 + b, nb + c)),
                _lead((j,), (conv_w.shape[1], tw), lambda b, c: (0, c)),
                _lead((j,), (1, tw), lambda b, c: (0, c)),
                _lead((j,), (2, 2, n_blk, LRU_BLOCK, LRU_BLOCK), lambda b, c: (0, 0, c, 0, 0)),
                _lead((j,), (2, 2, tw), lambda b, c: (0, 0, c)),
                _lead((j,), (2, tw), lambda b, c: (0, c))]
    args = [proj, proj, conv_w, conv_b[:, None, :], gate_w, gate_b, lam]
    if state is not None:
        in_specs.append(pl.BlockSpec((None, None, 2, tw), lambda b, c: (b, j, 0, c)))
        args.append(state)
    body = functools.partial(_lru_kernel, seq_len=seq_len, n_blk=n_blk,
                             has_state=state is not None)
    n_slabs = tw // LANES
    scr_rows = SUBLANES * (seq_len // SUBLANES + SEG_PAD)
    return pl.pallas_call(
        body,
        grid=(n_seq, nb),
        in_specs=in_specs,
        out_specs=[pl.BlockSpec((seq_len, tw), lambda b, c: (b, c)),
                   pl.BlockSpec((None, 2, tw), lambda b, c: (b, 0, c))],
        out_shape=[jax.ShapeDtypeStruct((n_seq * seq_len, width), BF16),
                   jax.ShapeDtypeStruct((n_seq, 2, width), F32)],
        scratch_shapes=[pltpu.VMEM((2, 2, n_slabs, scr_rows, LANES), F32),
                        pltpu.VMEM((2, 2, n_slabs, scr_rows, LANES), F32),
                        pltpu.VMEM((n_slabs, scr_rows, LANES), F32)],
        compiler_params=_params(2),
        name="lru_core",
    )(*args)


def _norm_kernel(x_ref, g_ref, o_ref):
    x = x_ref[...]
    o_ref[...] = (x * lax.rsqrt(jnp.mean(x * x, axis=-1, keepdims=True) + EPS)) * g_ref[...]


def _final_norm(x, gain, row0, rows, tr=512):
    d = x.shape[1]
    blk0 = row0 // tr
    return pl.pallas_call(
        _norm_kernel,
        grid=(rows // tr,),
        in_specs=[pl.BlockSpec((tr, d), lambda i: (blk0 + i, 0)),
                  pl.BlockSpec((1, d), lambda i: (0, 0))],
        out_specs=pl.BlockSpec((tr, d), lambda i: (i, 0)),
        out_shape=jax.ShapeDtypeStruct((rows, d), F32),
        compiler_params=_params(1),
        name="final_norm",
    )(x, gain.reshape(1, d))


def kernel(x_prompt, x_sample, state_ret, state_lru, c, c_ctx, norm_g, final_g, w_mod, b_mod,
           ffn_w_in, ffn_w_out, a_w_in, a_v_g, a_w_s, a_b_s, a_w_out, r_w_in, r_decay, r_gn_g,
           r_w_out, l_w_in, l_conv_w, l_conv_b, l_gate_w, l_gate_b, l_lambda, l_w_out):
    batch, seq, d = x_prompt.shape
    dec_batch, dec_seq, _ = x_sample.shape
    depth = w_mod.shape[0]
    m_prompt, m_sample = batch * seq, dec_batch * dec_seq
    assert m_prompt % TM == 0 and dec_seq % TM == 0 and dec_batch + 1 <= MOD_ROWS
    prompt_tiles, tiles_per_sample = m_prompt // TM, dec_seq // TM

    def seg(i):
        return jnp.maximum(i - prompt_tiles, -1) // tiles_per_sample + 1

    x = jnp.concatenate([x_prompt.reshape(m_prompt, d), x_sample.reshape(m_sample, d)], axis=0)
    cond = jnp.concatenate(
        [c_ctx[None, :], c, jnp.zeros((MOD_ROWS - 1 - dec_batch, d), F32)], axis=0)
    mod_all = _modulation(cond, w_mod, b_mod).reshape(depth, MOD_ROWS, N_MOD, d)
    rope = _rope_tables(dec_seq, d // R_HEADS)

    def ffn(x, l, s, mod, more_casts=()):
        casts = [(ffn_w_out, (l, s), CAST_ROWS)] + list(more_casts)
        hid, w_out, *rounded = _project(x, norm_g[l, 2 * s], mod, ffn_w_in, (l, s), seg,
                                        shift_idx=6 * s, scale_idx=6 * s + 1, act="swiglu",
                                        casts=casts)
        x = _out_project(hid, w_out, x, mod, seg, gate_idx=6 * s + 2, coef=0.5)
        return (x, *rounded)

    ret_states, lru_states = [], []
    for l in range(depth):
        kind, j = l % N_MIXERS, l // N_MIXERS
        mod = mod_all[l]
        mixer_w_in, mixer_w_out = ((a_w_in, a_w_out), (r_w_in, r_w_out), (l_w_in, l_w_out))[kind]
        x, w_in = ffn(x, l, 0, mod, [(mixer_w_in, (j,), MIX_CAST_ROWS)])
        proj, w_out = _project(x, norm_g[l, 1], mod, w_in, (), seg, shift_idx=3, scale_idx=4,
                               act="gelu" if kind == 0 else "none",
                               casts=[(mixer_w_out, (j,), CAST_ROWS)])
        if kind == 0:
            mixed = _gmlp_core(proj, a_v_g, a_w_s, a_b_s, j)
        elif kind == 1:
            o_p, st = _retention_core(proj, r_decay, r_gn_g, j, row0=0, n_seq=batch,
                                      seq_len=seq, state=None, rope=None, emit_state=True,
                                      hb=4)
            (o_s,) = _retention_core(proj, r_decay, r_gn_g, j, row0=m_prompt, n_seq=dec_batch,
                                     seq_len=dec_seq, state=state_ret, rope=rope,
                                     emit_state=False)
            mixed = (o_p, o_s)
            ret_states.append(st)
        else:
            o_p, st = _lru_core(proj, l_conv_w, l_conv_b, l_gate_w, l_gate_b, l_lambda, j, row0=0,
                                n_seq=batch, seq_len=seq, state=None)
            o_s, _ = _lru_core(proj, l_conv_w, l_conv_b, l_gate_w, l_gate_b, l_lambda, j,
                               row0=m_prompt, n_seq=dec_batch, seq_len=dec_seq, state=state_lru)
            mixed = (o_p, o_s)
            lru_states.append(st)
        x = _out_project(mixed, w_out, x, mod, seg, gate_idx=5, coef=1.0)
        (x,) = ffn(x, l, 1, mod)

    y_prompt = _final_norm(x, final_g, 0, m_prompt).reshape(batch, seq, d)
    y_sample = _final_norm(x, final_g, m_prompt, m_sample).reshape(dec_batch, dec_seq, d)
    new_state_ret = jnp.stack(ret_states, axis=1)
    new_state_lru = jnp.stack(lru_states, axis=1)
    return (y_prompt, y_sample, new_state_ret, new_state_lru)
```

```python
import functools

import jax
import jax.numpy as jnp
import numpy as np
from jax import lax
from jax.experimental import pallas as pl
from jax.experimental.pallas import tpu as pltpu

F32 = jnp.float32
BF16 = jnp.bfloat16

EPS = 1e-6
N_MIXERS = 3
N_MOD = 9
GRID_W = 64
A_CHUNK = 128
A_GROUPS = 8
R_HEADS = 8
R_CHUNK = 256
ROPE_BASE = 10000.0
LRU_BLOCK = 256
LRU_C = 8.0
GELU_C0 = float(np.sqrt(2.0 / np.pi).astype(np.float32))
GELU_C1 = 0.044715

LANES = 128
SUBLANES = 8
MXU_N = 256
SEG_PAD = 4
SCAN_UNROLL = 8
MOD_ROWS = SUBLANES
TM = 1024
ROW_CHUNK = 16
CAST_ROWS = 256
MIX_CAST_ROWS = 32
VMEM_LIMIT = 56 * 1024 * 1024
OUT_VMEM_BUDGET = 52 * 1024 * 1024
PROJ_VMEM_BUDGET = 52 * 1024 * 1024


def _params(n_grid):
    return pltpu.CompilerParams(dimension_semantics=("arbitrary",) * n_grid,
                                vmem_limit_bytes=VMEM_LIMIT)


def _lead(lead, block, index_map):
    return pl.BlockSpec((None,) * len(lead) + tuple(block),
                        lambda *g: tuple(lead) + tuple(index_map(*g)))


def _gelu_tanh(x):
    inner = x * (GELU_C0 + (GELU_C0 * GELU_C1) * (x * x))
    return (0.5 * x.astype(BF16)) * (1.0 + jnp.tanh(inner.astype(BF16)))


def _log_sigmoid(x):
    return jnp.minimum(x, 0.0) - jnp.log1p(jnp.exp(-jnp.abs(x)))


def _mod_kernel(c_ref, w_ref, b_ref, o_ref):
    s = jax.nn.silu(c_ref[...]).astype(BF16)
    o_ref[...] = jnp.dot(s, w_ref[...].astype(BF16), preferred_element_type=F32) + b_ref[...]


def _modulation(cond, w_mod, b_mod, tn=2048):
    depth, d, n = w_mod.shape
    return pl.pallas_call(
        _mod_kernel,
        grid=(depth, n // tn),
        in_specs=[pl.BlockSpec((MOD_ROWS, d), lambda l, j: (0, 0)),
                  pl.BlockSpec((None, d, tn), lambda l, j: (l, 0, j)),
                  pl.BlockSpec((None, 1, tn), lambda l, j: (l, 0, j))],
        out_specs=pl.BlockSpec((None, MOD_ROWS, tn), lambda l, j: (l, 0, j)),
        out_shape=jax.ShapeDtypeStruct((depth, MOD_ROWS, n), F32),
        compiler_params=_params(2),
        name="modulation",
    )(cond, w_mod, b_mod.reshape(depth, 1, n))


def _round_chunk(src_ref, dst_ref):
    dst_ref[...] = src_ref[...].astype(BF16)


def _proj_kernel(x_ref, g_ref, mod_ref, *refs, n_w, cast_steps, shift_idx, scale_idx, act):
    n_c = len(cast_steps)
    w_refs, wc_refs = refs[:n_w], refs[n_w:n_w + n_c]
    o_ref, wb_refs, h_ref = refs[n_w + n_c], refs[n_w + n_c + 1:-1], refs[-1]

    @pl.when(pl.program_id(1) == 0)
    def _prologue():
        gain = g_ref[...]
        scale = 1.0 + mod_ref[scale_idx:scale_idx + 1, :]
        shift = mod_ref[shift_idx:shift_idx + 1, :]

        def rows_body(r, carry):
            rows = pl.ds(pl.multiple_of(r * ROW_CHUNK, ROW_CHUNK), ROW_CHUNK)
            x = x_ref[rows, :]
            y = x * lax.rsqrt(jnp.mean(x * x, axis=-1, keepdims=True) + EPS)
            h_ref[rows, :] = ((y * gain) * scale + shift).astype(h_ref.dtype)
            return carry

        lax.fori_loop(0, x_ref.shape[0] // ROW_CHUNK, rows_body, 0, unroll=8)

    step = pl.program_id(0) * pl.num_programs(1) + pl.program_id(1)
    for wc_ref, wb_ref, n_cast in zip(wc_refs, wb_refs, cast_steps):
        pl.when(step < n_cast)(functools.partial(_round_chunk, wc_ref, wb_ref))

    h = h_ref[...]
    for c in range(0, o_ref.shape[1], MXU_N):
        cols = slice(c, c + MXU_N)
        ys = [jnp.dot(h, w[:, cols].astype(BF16), preferred_element_type=F32) for w in w_refs]
        if act == "swiglu":
            out = jax.nn.silu(ys[0]) * ys[1]
        elif act == "gelu":
            out = _gelu_tanh(ys[0])
        else:
            out = ys[0]
        o_ref[:, cols] = out.astype(o_ref.dtype)


def _proj_tile_n(d, n_out, n_w, w_itemsize, cast_bytes):
    fixed = 2 * TM * d * 4 + TM * d * 2 + cast_bytes
    for k in range(n_out // MXU_N, 0, -1):
        tn = k * MXU_N
        blocks = fixed + 2 * (n_w * d * tn * w_itemsize + TM * tn * 2)
        if n_out % tn == 0 and blocks <= PROJ_VMEM_BUDGET:
            return tn
    raise ValueError("projection tiles do not fit VMEM")


def _project(x, gain, mod, w, lead, seg, *, shift_idx, scale_idx, act, casts=()):
    m, d = x.shape
    n_w = 2 if act == "swiglu" else 1
    n_out = w.shape[-1] // n_w
    cast_bytes = sum(2 * rows * wn.shape[-1] * (wn.dtype.itemsize + 2) for wn, _, rows in casts)
    tn = _proj_tile_n(d, n_out, n_w, w.dtype.itemsize, cast_bytes)
    grid = (m // TM, n_out // tn)
    w_specs = [_lead(lead, (d, tn), lambda i, j: (0, j))]
    if act == "swiglu":
        w_specs.append(_lead(lead, (d, tn), lambda i, j: (0, j + grid[1])))
    in_specs = [pl.BlockSpec((TM, d), lambda i, j: (i, 0)),
                pl.BlockSpec((1, d), lambda i, j: (0, 0)),
                pl.BlockSpec((None, N_MOD, d), lambda i, j: (seg(i), 0, 0))] + w_specs
    args = [x, gain.reshape(1, d), mod] + [w] * n_w
    out_specs = [pl.BlockSpec((TM, tn), lambda i, j: (i, j))]
    out_shape = [jax.ShapeDtypeStruct((m, n_out), BF16)]
    cast_steps = []
    for w_next, lead_next, rows in casts:
        k_next, d_next = w_next.shape[-2:]
        n_cast = k_next // rows
        assert k_next % rows == 0 and n_cast <= grid[0] * grid[1]

        def chunk(i, j, n_cast=n_cast):
            return (jnp.minimum(i * grid[1] + j, n_cast - 1), 0)

        in_specs.append(_lead(lead_next, (rows, d_next), chunk))
        args.append(w_next)
        out_specs.append(pl.BlockSpec((rows, d_next), chunk))
        out_shape.append(jax.ShapeDtypeStruct((k_next, d_next), BF16))
        cast_steps.append(n_cast)
    body = functools.partial(_proj_kernel, n_w=n_w, cast_steps=tuple(cast_steps),
                             shift_idx=shift_idx, scale_idx=scale_idx, act=act)
    return pl.pallas_call(
        body,
        grid=grid,
        in_specs=in_specs,
        out_specs=out_specs,
        out_shape=out_shape,
        scratch_shapes=[pltpu.VMEM((TM, d), BF16)],
        compiler_params=_params(2),
        name="project_" + act,
    )(*args)


def _out_kernel(*refs, n_a, split, gate_idx, coef):
    a_refs = refs[:n_a]
    w_ref, x_ref, mod_ref, o_ref = refs[n_a:]

    def update(a_ref):
        y = jnp.dot(a_ref[...], w_ref[...], preferred_element_type=F32)
        gate = mod_ref[gate_idx:gate_idx + 1, :]
        if coef != 1.0:
            gate = coef * gate
        o_ref[...] = x_ref[...] + gate * y

    if n_a == 1:
        update(a_refs[0])
    else:
        pl.when(pl.program_id(0) < split)(functools.partial(update, a_refs[0]))
        pl.when(pl.program_id(0) >= split)(functools.partial(update, a_refs[1]))


def _out_tile_n(n_a, k, d):
    for tn in (2048, 1024, 512, 256):
        blocks = 2 * (n_a * TM * k * 2 + k * tn * 2 + 2 * TM * tn * 4) + TM * tn * 4
        if d % tn == 0 and blocks <= OUT_VMEM_BUDGET:
            return tn
    raise ValueError("out_project tiles do not fit VMEM")


def _out_project(a, w, x, mod, seg, *, gate_idx, coef):
    m, d = x.shape
    assert w.dtype == BF16
    tn = _out_tile_n(len(a) if isinstance(a, tuple) else 1, w.shape[0], d)
    if isinstance(a, tuple):
        split = a[0].shape[0] // TM
        k = a[0].shape[1]
        a_specs = [pl.BlockSpec((TM, k), lambda i, j: (jnp.minimum(i, split - 1), 0)),
                   pl.BlockSpec((TM, k), lambda i, j: (jnp.maximum(i - split, 0), 0))]
    else:
        split, k = 0, a.shape[1]
        a_specs = [pl.BlockSpec((TM, k), lambda i, j: (i, 0))]
        a = (a,)
    body = functools.partial(_out_kernel, n_a=len(a), split=split, gate_idx=gate_idx, coef=coef)
    return pl.pallas_call(
        body,
        grid=(m // TM, d // tn),
        in_specs=a_specs + [pl.BlockSpec((k, tn), lambda i, j: (0, j)),
                            pl.BlockSpec((TM, tn), lambda i, j: (i, j)),
                            pl.BlockSpec((None, N_MOD, tn), lambda i, j: (seg(i), 0, j))],
        out_specs=pl.BlockSpec((TM, tn), lambda i, j: (i, j)),
        out_shape=jax.ShapeDtypeStruct((m, d), F32),
        compiler_params=_params(2),
        name="out_project",
    )(*a, w, x, mod)


def _gmlp_kernel(u_ref, v_ref, vg_ref, ws_ref, bs_ref, o_ref):
    width = v_ref.shape[1]
    gw = width // A_GROUPS
    vg = vg_ref[...]
    for c in range(v_ref.shape[0] // A_CHUNK):
        rows = slice(c * A_CHUNK, (c + 1) * A_CHUNK)
        v = v_ref[rows, :].astype(F32)
        vn = v * lax.rsqrt(jnp.mean(v * v, axis=-1, keepdims=True) + EPS)
        vn = (vn * vg).astype(BF16)
        for g in range(A_GROUPS):
            cols = slice(g * gw, (g + 1) * gw)
            sv = jnp.dot(ws_ref[g].astype(BF16), vn[:, cols], preferred_element_type=F32)
            sv = sv + bs_ref[:, g:g + 1]
            o_ref[rows, cols] = (u_ref[rows, cols].astype(F32) * sv).astype(o_ref.dtype)


def _gmlp_core(uv, v_g, w_s, b_s, j, rows_per_step=4 * A_CHUNK):
    m, two_w = uv.shape
    width = two_w // 2
    tr = rows_per_step
    return pl.pallas_call(
        _gmlp_kernel,
        grid=(m // tr,),
        in_specs=[pl.BlockSpec((tr, width), lambda i: (i, 0)),
                  pl.BlockSpec((tr, width), lambda i: (i, 1)),
                  _lead((j,), (1, width), lambda i: (0, 0)),
                  _lead((j,), (A_GROUPS, A_CHUNK, A_CHUNK), lambda i: (0, 0, 0)),
                  _lead((j,), (A_CHUNK, A_GROUPS), lambda i: (0, 0))],
        out_specs=pl.BlockSpec((tr, width), lambda i: (i, 0)),
        out_shape=jax.ShapeDtypeStruct((m, width), BF16),
        compiler_params=_params(1),
        name="gmlp_core",
    )(uv, uv, v_g[:, None, :], w_s, jnp.swapaxes(b_s, 1, 2))


def _ret_kernel(*refs, seq_len, dk, hb, has_state, has_rope, emit_state):
    it = iter(refs)
    q_ref, k_ref, v_ref, g_ref, dec_ref, gn_ref = (next(it) for _ in range(6))
    cos_ref, sin_ref = (next(it), next(it)) if has_rope else (None, None)
    s0_ref = next(it) if has_state else None
    o_ref = next(it)
    st_ref = next(it) if emit_state else None
    q_scr, k_scr, acc_scr, r_scr = (next(it) for _ in range(4))
    chunk = R_CHUNK
    n_chunks = seq_len // chunk
    dv = 2 * dk

    ri = lax.broadcasted_iota(jnp.int32, (chunk, chunk), 0).astype(F32)
    ci = lax.broadcasted_iota(jnp.int32, (chunk, chunk), 1).astype(F32)
    rel = ri - ci

    def head(hh):
        qk_cols = slice(hh * dk, (hh + 1) * dk)
        v_cols = slice(hh * dv, (hh + 1) * dv)
        q = q_ref[:, qk_cols].astype(F32)
        k = k_ref[:, qk_cols].astype(F32)
        if has_rope:
            cos, sin = cos_ref[...], sin_ref[...]

            def rope(t):
                swapped = jnp.concatenate(
                    [pltpu.roll(t[:, s:s + LANES], LANES // 2, 1) for s in range(0, dk, LANES)],
                    axis=1)
                return t * cos + swapped * sin

            q, k = rope(q), rope(k)
        q_scr[:, qk_cols] = q.astype(BF16)
        k_scr[:, qk_cols] = k * (dk ** -0.5)

        consts = []
        for direction in range(2):
            lg = _log_sigmoid(dec_ref[direction, hh])
            if direction == 0:
                dmat = jnp.where(rel >= 0, jnp.exp(jnp.maximum(rel, 0.0) * lg), 0.0)
                zeta = jnp.exp((chunk - 1 - ri) * lg)[:, :1]
                xi = jnp.exp((ri + 1) * lg)[:, :1]
            else:
                dmat = jnp.where(rel <= 0, jnp.exp(jnp.maximum(-rel, 0.0) * lg), 0.0)
                zeta = jnp.exp(ri * lg)[:, :1]
                xi = jnp.exp((chunk - ri) * lg)[:, :1]
            consts.append((dmat, zeta, xi, jnp.exp(chunk * lg)[:, :1]))
            if has_state:
                r_scr[direction, hh] = s0_ref[direction, hh].astype(F32)
            else:
                r_scr[direction, hh] = jnp.zeros((dk, dv), F32)

        def chunk_step(direction, i):
            dmat, zeta, xi, g_chunk = consts[direction]
            rows = slice(i * chunk, (i + 1) * chunk)
            qi = q_scr[rows, qk_cols]
            kf = k_scr[rows, qk_cols]
            vi = v_ref[rows, v_cols]
            s = lax.dot_general(qi, kf.astype(BF16), (((1,), (1,)), ((), ())),
                                preferred_element_type=F32)
            o_inner = jnp.dot((s * dmat).astype(BF16), vi, preferred_element_type=F32)
            o_cross = jnp.dot(qi, r_scr[direction, hh].astype(BF16),
                              preferred_element_type=F32) * xi
            kv = lax.dot_general((kf * zeta).astype(BF16), vi, (((0,), (0,)), ((), ())),
                                 preferred_element_type=F32)
            r_scr[direction, hh] = g_chunk * r_scr[direction, hh] + kv
            acc_scr[direction, rows, v_cols] = o_inner + o_cross

        for step in range(n_chunks):
            chunk_step(0, step)
            chunk_step(1, n_chunks - 1 - step)

        gn = gn_ref[:, v_cols]
        for i in range(n_chunks):
            rows = slice(i * chunk, (i + 1) * chunk)
            o = acc_scr[0, rows, v_cols] + acc_scr[1, rows, v_cols]
            mu = jnp.mean(o, axis=-1, keepdims=True)
            var = jnp.mean(jnp.square(o - mu), axis=-1, keepdims=True)
            on = ((o - mu) * lax.rsqrt(var + EPS)) * gn
            o_ref[rows, v_cols] = (jax.nn.silu(g_ref[rows, v_cols].astype(F32)) * on
                                   ).astype(o_ref.dtype)

    for hh in range(hb):
        head(hh)
    if emit_state:
        st_ref[...] = r_scr[...]


def _retention_core(proj, decay, gn_g, j, *, row0, n_seq, seq_len, state, rope, emit_state,
                    hb=2):
    heads = R_HEADS
    dk = proj.shape[1] // (6 * heads)
    dv = 2 * dk
    blk0 = row0 // seq_len
    hg = heads // hb
    dec = jnp.broadcast_to(decay[j][:, :, None, None], (2, heads, 1, R_CHUNK))
    in_specs = [pl.BlockSpec((seq_len, hb * dk), lambda b, h: (blk0 + b, h)),
                pl.BlockSpec((seq_len, hb * dk), lambda b, h: (blk0 + b, hg + h)),
                pl.BlockSpec((seq_len, hb * dv), lambda b, h: (blk0 + b, hg + h)),
                pl.BlockSpec((seq_len, hb * dv), lambda b, h: (blk0 + b, 2 * hg + h)),
                pl.BlockSpec((2, hb, 1, R_CHUNK), lambda b, h: (0, h, 0, 0)),
                _lead((j,), (1, hb * dv), lambda b, h: (0, h))]
    args = [proj, proj, proj, proj, dec, gn_g[:, None, :]]
    if rope is not None:
        in_specs += [pl.BlockSpec((seq_len, dk), lambda b, h: (0, 0))] * 2
        args += list(rope)
    if state is not None:
        in_specs.append(pl.BlockSpec((None, None, 2, hb, dk, dv),
                                     lambda b, h: (b, j, 0, h, 0, 0)))
        args.append(state)
    out_specs = [pl.BlockSpec((seq_len, hb * dv), lambda b, h: (b, h))]
    out_shape = [jax.ShapeDtypeStruct((n_seq * seq_len, heads * dv), BF16)]
    if emit_state:
        out_specs.append(pl.BlockSpec((None, 2, hb, dk, dv), lambda b, h: (b, 0, h, 0, 0)))
        out_shape.append(jax.ShapeDtypeStruct((n_seq, 2, heads, dk, dv), F32))
    body = functools.partial(_ret_kernel, seq_len=seq_len, dk=dk, hb=hb,
                             has_state=state is not None, has_rope=rope is not None,
                             emit_state=emit_state)
    return pl.pallas_call(
        body,
        grid=(n_seq, hg),
        in_specs=in_specs,
        out_specs=out_specs,
        out_shape=out_shape,
        scratch_shapes=[pltpu.VMEM((seq_len, hb * dk), BF16),
                        pltpu.VMEM((seq_len, hb * dk), F32),
                        pltpu.VMEM((2, seq_len, hb * dv), F32),
                        pltpu.VMEM((2, hb, dk, dv), F32)],
        compiler_params=_params(2),
        name="retention_core",
    )(*args)


def _rope_tables(seq_len, dk):
    nf = dk // 4
    inv = ROPE_BASE ** (-jnp.arange(nf, dtype=F32) / nf)
    t = jnp.arange(seq_len)
    tabs = []
    for p in (t // GRID_W, t % GRID_W):
        ang = p.astype(F32)[:, None] * inv
        tabs.append((jnp.cos(ang), jnp.sin(ang)))
    cos = jnp.concatenate([tabs[0][0], tabs[0][0], tabs[1][0], tabs[1][0]], axis=1)
    sin = jnp.concatenate([-tabs[0][1], tabs[0][1], -tabs[1][1], tabs[1][1]], axis=1)
    return cos, sin


def _lru_kernel(*refs, seq_len, n_blk, has_state):
    it = iter(refs)
    y_ref, x_ref, cw_ref, cb_ref, gw_ref, gb_ref, lam_ref = (next(it) for _ in range(7))
    h0_ref = next(it) if has_state else None
    o_ref, st_ref, ab_scr, run_scr, h_scr = (next(it) for _ in range(5))
    n = seq_len
    row = lax.broadcasted_iota(jnp.int32, (n, LRU_BLOCK), 0)

    def shifted(v, d, fill):
        if d == 0:
            return v
        if d > 0:
            return jnp.where(row >= d, pltpu.roll(v, d, 0), fill)
        return jnp.where(row < n + d, pltpu.roll(v, n + d, 0), fill)

    seg = n // SUBLANES
    pitch = seg + SEG_PAD
    n_slabs = x_ref.shape[1] // LANES
    slabs_per_block = LRU_BLOCK // LANES
    conv_w = cw_ref.shape[0]
    pad_l = conv_w // 2
    for blk in range(n_blk):
        bc = slice(blk * LRU_BLOCK, (blk + 1) * LRU_BLOCK)
        x = x_ref[:, bc].astype(F32)
        xc = shifted(x, pad_l, 0.0) * cw_ref[0:1, bc]
        for t in range(1, conv_w):
            xc = xc + shifted(x, pad_l - t, 0.0) * cw_ref[t:t + 1, bc]
        xc = xc + cb_ref[:, bc]
        xcb = xc.astype(BF16)
        for direction in range(2):
            gates = [jax.nn.sigmoid(
                jnp.dot(xcb, gw_ref[direction, gi, blk].astype(BF16), preferred_element_type=F32)
                + gb_ref[direction, gi:gi + 1, bc]) for gi in range(2)]
            log_a = (LRU_C * gates[0]) * _log_sigmoid(lam_ref[direction:direction + 1, bc])
            a = jnp.exp(log_a)
            b = (jnp.sqrt(-jnp.tanh(log_a) * (a * a + 1.0)) * gates[1]) * xc
            for sl in range(slabs_per_block):
                slab = blk * slabs_per_block + sl
                cols = slice(sl * LANES, (sl + 1) * LANES)
                for s in range(SUBLANES):
                    src, dst = slice(s * seg, (s + 1) * seg), slice(s * pitch, s * pitch + seg)
                    ab_scr[direction, 0, slab, dst, :] = a[src, cols]
                    ab_scr[direction, 1, slab, dst, :] = b[src, cols]

    def step_rows(i):
        return pl.ds(i, SUBLANES, stride=pitch)

    chains = [(d, sl) for d in range(2) for sl in range(n_slabs)]

    def scan_body(i, carry):
        out = []
        for (d, sl), (a_run, h_run) in zip(chains, carry):
            rows = step_rows(i if d == 0 else seg - 1 - i)
            a_i = ab_scr[d, 0, sl, rows, :]
            h_run = a_i * h_run + ab_scr[d, 1, sl, rows, :]
            a_run = a_i * a_run
            run_scr[d, 0, sl, rows, :] = a_run
            run_scr[d, 1, sl, rows, :] = h_run
            out.append((a_run, h_run))
        return tuple(out)

    init = tuple((jnp.ones((SUBLANES, LANES), F32), jnp.zeros((SUBLANES, LANES), F32))
                 for _ in chains)
    ends = lax.fori_loop(0, seg, scan_body, init, unroll=SCAN_UNROLL)

    entering = []
    for (d, sl), (a_end, h_end) in zip(chains, ends):
        cols = slice(sl * LANES, (sl + 1) * LANES)
        c = h0_ref[d:d + 1, cols] if has_state else jnp.zeros((1, LANES), F32)
        rows = [None] * SUBLANES
        for s in (range(SUBLANES) if d == 0 else range(SUBLANES - 1, -1, -1)):
            rows[s] = c
            c = h_end[s:s + 1, :] + a_end[s:s + 1, :] * c
        st_ref[d:d + 1, cols] = c
        entering.append(jnp.concatenate(rows, axis=0))

    def fix_body(i, carry):
        rows = step_rows(i)
        for sl in range(n_slabs):
            h_f, h_b = (run_scr[d, 1, sl, rows, :]
                        + run_scr[d, 0, sl, rows, :] * entering[chains.index((d, sl))]
                        for d in range(2))
            h_scr[sl, rows, :] = h_f + h_b
        return carry

    lax.fori_loop(0, seg, fix_body, 0, unroll=SCAN_UNROLL)

    for sl in range(n_slabs):
        cols = slice(sl * LANES, (sl + 1) * LANES)
        for s in range(SUBLANES):
            src, dst = slice(s * pitch, s * pitch + seg), slice(s * seg, (s + 1) * seg)
            o_ref[dst, cols] = (h_scr[sl, src, :]
                                * jax.nn.gelu(y_ref[dst, cols].astype(F32))).astype(o_ref.dtype)


def _lru_core(proj, conv_w, conv_b, gate_w, gate_b, lam, j, *, row0, n_seq, seq_len, state,
              n_blk=2):
    width = proj.shape[1] // 2
    tw = n_blk * LRU_BLOCK
    nb = width // tw
    blk0 = row0 // seq_len
    in_specs = [pl.BlockSpec((seq_len, tw), lambda b, c: (blk0 + b, c)),
                pl.BlockSpec((seq_len, tw), lambda b, c: (blk0 + b, nb + c)),
                _lead((j,), (conv_w.shape[1], tw), lambda b, c: (0, c)),
                _lead((j,), (1, tw), lambda b, c: (0, c)),
                _lead((j,), (2, 2, n_blk, LRU_BLOCK, LRU_BLOCK), lambda b, c: (0, 0, c, 0, 0)),
                _lead((j,), (2, 2, tw), lambda b, c: (0, 0, c)),
                _lead((j,), (2, tw), lambda b, c: (0, c))]
    args = [proj, proj, conv_w, conv_b[:, None, :], gate_w, gate_b, lam]
    if state is not None:
        in_specs.append(pl.BlockSpec((None, None, 2, tw), lambda b, c: (b, j, 0, c)))
        args.append(state)
    body = functools.partial(_lru_kernel, seq_len=seq_len, n_blk=n_blk,
                             has_state=state is not None)
    n_slabs = tw // LANES
    scr_rows = SUBLANES * (seq_len // SUBLANES + SEG_PAD)
    return pl.pallas_call(
        body,
        grid=(n_seq, nb),
        in_specs=in_specs,
        out_specs=[pl.BlockSpec((seq_len, tw), lambda b, c: (b, c)),
                   pl.BlockSpec((None, 2, tw), lambda b, c: (b, 0, c))],
        out_shape=[jax.ShapeDtypeStruct((n_seq * seq_len, width), BF16),
                   jax.ShapeDtypeStruct((n_seq, 2, width), F32)],
        scratch_shapes=[pltpu.VMEM((2, 2, n_slabs, scr_rows, LANES), F32),
                        pltpu.VMEM((2, 2, n_slabs, scr_rows, LANES), F32),
                        pltpu.VMEM((n_slabs, scr_rows, LANES), F32)],
        compiler_params=_params(2),
        name="lru_core",
    )(*args)


def _norm_kernel(x_ref, g_ref, o_ref):
    x = x_ref[...]
    o_ref[...] = (x * lax.rsqrt(jnp.mean(x * x, axis=-1, keepdims=True) + EPS)) * g_ref[...]


def _final_norm(x, gain, row0, rows, tr=512):
    d = x.shape[1]
    blk0 = row0 // tr
    return pl.pallas_call(
        _norm_kernel,
        grid=(rows // tr,),
        in_specs=[pl.BlockSpec((tr, d), lambda i: (blk0 + i, 0)),
                  pl.BlockSpec((1, d), lambda i: (0, 0))],
        out_specs=pl.BlockSpec((tr, d), lambda i: (i, 0)),
        out_shape=jax.ShapeDtypeStruct((rows, d), F32),
        compiler_params=_params(1),
        name="final_norm",
    )(x, gain.reshape(1, d))


def kernel(x_prompt, x_sample, state_ret, state_lru, c, c_ctx, norm_g, final_g, w_mod, b_mod,
           ffn_w_in, ffn_w_out, a_w_in, a_v_g, a_w_s, a_b_s, a_w_out, r_w_in, r_decay, r_gn_g,
           r_w_out, l_w_in, l_conv_w, l_conv_b, l_gate_w, l_gate_b, l_lambda, l_w_out):
    batch, seq, d = x_prompt.shape
    dec_batch, dec_seq, _ = x_sample.shape
    depth = w_mod.shape[0]
    m_prompt, m_sample = batch * seq, dec_batch * dec_seq
    assert m_prompt % TM == 0 and dec_seq % TM == 0 and dec_batch + 1 <= MOD_ROWS
    prompt_tiles, tiles_per_sample = m_prompt // TM, dec_seq // TM

    def seg(i):
        return jnp.maximum(i - prompt_tiles, -1) // tiles_per_sample + 1

    x = jnp.concatenate([x_prompt.reshape(m_prompt, d), x_sample.reshape(m_sample, d)], axis=0)
    cond = jnp.concatenate(
        [c_ctx[None, :], c, jnp.zeros((MOD_ROWS - 1 - dec_batch, d), F32)], axis=0)
    mod_all = _modulation(cond, w_mod, b_mod).reshape(depth, MOD_ROWS, N_MOD, d)
    rope = _rope_tables(dec_seq, d // R_HEADS)

    def ffn(x, l, s, mod, more_casts=()):
        casts = [(ffn_w_out, (l, s), CAST_ROWS)] + list(more_casts)
        hid, w_out, *rounded = _project(x, norm_g[l, 2 * s], mod, ffn_w_in, (l, s), seg,
                                        shift_idx=6 * s, scale_idx=6 * s + 1, act="swiglu",
                                        casts=casts)
        x = _out_project(hid, w_out, x, mod, seg, gate_idx=6 * s + 2, coef=0.5)
        return (x, *rounded)

    ret_states, lru_states = [], []
    for l in range(depth):
        kind, j = l % N_MIXERS, l // N_MIXERS
        mod = mod_all[l]
        mixer_w_in, mixer_w_out = ((a_w_in, a_w_out), (r_w_in, r_w_out), (l_w_in, l_w_out))[kind]
        x, w_in = ffn(x, l, 0, mod, [(mixer_w_in, (j,), MIX_CAST_ROWS)])
        proj, w_out = _project(x, norm_g[l, 1], mod, w_in, (), seg, shift_idx=3, scale_idx=4,
                               act="gelu" if kind == 0 else "none",
                               casts=[(mixer_w_out, (j,), CAST_ROWS)])
        if kind == 0:
            mixed = _gmlp_core(proj, a_v_g, a_w_s, a_b_s, j)
        elif kind == 1:
            o_p, st = _retention_core(proj, r_decay, r_gn_g, j, row0=0, n_seq=batch,
                                      seq_len=seq, state=None, rope=None, emit_state=True,
                                      hb=4)
            (o_s,) = _retention_core(proj, r_decay, r_gn_g, j, row0=m_prompt, n_seq=dec_batch,
                                     seq_len=dec_seq, state=state_ret, rope=rope,
                                     emit_state=False)
            mixed = (o_p, o_s)
            ret_states.append(st)
        else:
            o_p, st = _lru_core(proj, l_conv_w, l_conv_b, l_gate_w, l_gate_b, l_lambda, j, row0=0,
                                n_seq=batch, seq_len=seq, state=None)
            o_s, _ = _lru_core(proj, l_conv_w, l_conv_b, l_gate_w, l_gate_b, l_lambda, j,
                               row0=m_prompt, n_seq=dec_batch, seq_len=dec_seq, state=state_lru)
            mixed = (o_p, o_s)
            lru_states.append(st)
        x = _out_project(mixed, w_out, x, mod, seg, gate_idx=5, coef=1.0)
        (x,) = ffn(x, l, 1, mod)

    y_prompt = _final_norm(x, final_g, 0, m_prompt).reshape(batch, seq, d)
    y_sample = _final_norm(x, final_g, m_prompt, m_sample).reshape(dec_batch, dec_seq, d)
    new_state_ret = jnp.stack(ret_states, axis=1)
    new_state_lru = jnp.stack(lru_states, axis=1)
    return (y_prompt, y_sample, new_state_ret, new_state_lru)
```

```python
import functools

import jax
import jax.numpy as jnp
import numpy as np
from jax import lax
from jax.experimental import pallas as pl
from jax.experimental.pallas import tpu as pltpu

F32 = jnp.float32
BF16 = jnp.bfloat16

EPS = 1e-6
N_MIXERS = 3
N_MOD = 9
GRID_W = 64
A_CHUNK = 128
A_GROUPS = 8
R_HEADS = 8
R_CHUNK = 256
ROPE_BASE = 10000.0
LRU_BLOCK = 256
LRU_C = 8.0
GELU_C0 = float(np.sqrt(2.0 / np.pi).astype(np.float32))
GELU_C1 = 0.044715

LANES = 128
SUBLANES = 8
MXU_N = 256
SEG_PAD = 4
SCAN_UNROLL = 8
MOD_ROWS = SUBLANES
TM = 1024
ROW_CHUNK = 16
CAST_ROWS = 256
MIX_CAST_ROWS = 32
VMEM_LIMIT = 56 * 1024 * 1024
OUT_VMEM_BUDGET = 52 * 1024 * 1024
PROJ_VMEM_BUDGET = 52 * 1024 * 1024


def _params(n_grid):
    return pltpu.CompilerParams(dimension_semantics=("arbitrary",) * n_grid,
                                vmem_limit_bytes=VMEM_LIMIT)


def _lead(lead, block, index_map):
    return pl.BlockSpec((None,) * len(lead) + tuple(block),
                        lambda *g: tuple(lead) + tuple(index_map(*g)))


def _gelu_tanh(x):
    inner = x * (GELU_C0 + (GELU_C0 * GELU_C1) * (x * x))
    return (0.5 * x.astype(BF16)) * (1.0 + jnp.tanh(inner.astype(BF16)))


def _log_sigmoid(x):
    return jnp.minimum(x, 0.0) - jnp.log1p(jnp.exp(-jnp.abs(x)))


def _mod_kernel(c_ref, w_ref, b_ref, o_ref):
    s = jax.nn.silu(c_ref[...]).astype(BF16)
    o_ref[...] = jnp.dot(s, w_ref[...].astype(BF16), preferred_element_type=F32) + b_ref[...]


def _modulation(cond, w_mod, b_mod, tn=2048):
    depth, d, n = w_mod.shape
    return pl.pallas_call(
        _mod_kernel,
        grid=(depth, n // tn),
        in_specs=[pl.BlockSpec((MOD_ROWS, d), lambda l, j: (0, 0)),
                  pl.BlockSpec((None, d, tn), lambda l, j: (l, 0, j)),
                  pl.BlockSpec((None, 1, tn), lambda l, j: (l, 0, j))],
        out_specs=pl.BlockSpec((None, MOD_ROWS, tn), lambda l, j: (l, 0, j)),
        out_shape=jax.ShapeDtypeStruct((depth, MOD_ROWS, n), F32),
        compiler_params=_params(2),
        name="modulation",
    )(cond, w_mod, b_mod.reshape(depth, 1, n))


def _round_chunk(src_ref, dst_ref):
    dst_ref[...] = src_ref[...].astype(BF16)


def _proj_kernel(x_ref, g_ref, mod_ref, *refs, n_w, cast_steps, shift_idx, scale_idx, act):
    n_c = len(cast_steps)
    w_refs, wc_refs = refs[:n_w], refs[n_w:n_w + n_c]
    o_ref, wb_refs, h_ref = refs[n_w + n_c], refs[n_w + n_c + 1:-1], refs[-1]

    @pl.when(pl.program_id(1) == 0)
    def _prologue():
        gain = g_ref[...] * (1.0 + mod_ref[scale_idx:scale_idx + 1, :])
        shift = mod_ref[shift_idx:shift_idx + 1, :]

        def rows_body(r, carry):
            rows = pl.ds(pl.multiple_of(r * ROW_CHUNK, ROW_CHUNK), ROW_CHUNK)
            x = x_ref[rows, :]
            y = x * lax.rsqrt(jnp.mean(x * x, axis=-1, keepdims=True) + EPS)
            h_ref[rows, :] = (y * gain + shift).astype(h_ref.dtype)
            return carry

        lax.fori_loop(0, x_ref.shape[0] // ROW_CHUNK, rows_body, 0, unroll=8)

    step = pl.program_id(0) * pl.num_programs(1) + pl.program_id(1)
    for wc_ref, wb_ref, n_cast in zip(wc_refs, wb_refs, cast_steps):
        pl.when(step < n_cast)(functools.partial(_round_chunk, wc_ref, wb_ref))

    h = h_ref[...]
    for c in range(0, o_ref.shape[1], MXU_N):
        cols = slice(c, c + MXU_N)
        ys = [jnp.dot(h, w[:, cols].astype(BF16), preferred_element_type=F32) for w in w_refs]
        if act == "swiglu":
            out = jax.nn.silu(ys[0]) * ys[1]
        elif act == "gelu":
            out = _gelu_tanh(ys[0])
        else:
            out = ys[0]
        o_ref[:, cols] = out.astype(o_ref.dtype)


def _proj_tile_n(d, n_out, n_w, w_itemsize, cast_bytes):
    fixed = 2 * TM * d * 4 + TM * d * 2 + cast_bytes
    for k in range(n_out // MXU_N, 0, -1):
        tn = k * MXU_N
        blocks = fixed + 2 * (n_w * d * tn * w_itemsize + TM * tn * 2)
        if n_out % tn == 0 and blocks <= PROJ_VMEM_BUDGET:
            return tn
    raise ValueError("projection tiles do not fit VMEM")


def _project(x, gain, mod, w, lead, seg, *, shift_idx, scale_idx, act, casts=()):
    m, d = x.shape
    n_w = 2 if act == "swiglu" else 1
    n_out = w.shape[-1] // n_w
    cast_bytes = sum(2 * rows * wn.shape[-1] * (wn.dtype.itemsize + 2) for wn, _, rows in casts)
    tn = _proj_tile_n(d, n_out, n_w, w.dtype.itemsize, cast_bytes)
    grid = (m // TM, n_out // tn)
    w_specs = [_lead(lead, (d, tn), lambda i, j: (0, j))]
    if act == "swiglu":
        w_specs.append(_lead(lead, (d, tn), lambda i, j: (0, j + grid[1])))
    in_specs = [pl.BlockSpec((TM, d), lambda i, j: (i, 0)),
                pl.BlockSpec((1, d), lambda i, j: (0, 0)),
                pl.BlockSpec((None, N_MOD, d), lambda i, j: (seg(i), 0, 0))] + w_specs
    args = [x, gain.reshape(1, d), mod] + [w] * n_w
    out_specs = [pl.BlockSpec((TM, tn), lambda i, j: (i, j))]
    out_shape = [jax.ShapeDtypeStruct((m, n_out), BF16)]
    cast_steps = []
    for w_next, lead_next, rows in casts:
        k_next, d_next = w_next.shape[-2:]
        n_cast = k_next // rows
        assert k_next % rows == 0 and n_cast <= grid[0] * grid[1]

        def chunk(i, j, n_cast=n_cast):
            return (jnp.minimum(i * grid[1] + j, n_cast - 1), 0)

        in_specs.append(_lead(lead_next, (rows, d_next), chunk))
        args.append(w_next)
        out_specs.append(pl.BlockSpec((rows, d_next), chunk))
        out_shape.append(jax.ShapeDtypeStruct((k_next, d_next), BF16))
        cast_steps.append(n_cast)
    body = functools.partial(_proj_kernel, n_w=n_w, cast_steps=tuple(cast_steps),
                             shift_idx=shift_idx, scale_idx=scale_idx, act=act)
    return pl.pallas_call(
        body,
        grid=grid,
        in_specs=in_specs,
        out_specs=out_specs,
        out_shape=out_shape,
        scratch_shapes=[pltpu.VMEM((TM, d), BF16)],
        compiler_params=_params(2),
        name="project_" + act,
    )(*args)


def _out_kernel(*refs, n_a, split, gate_idx, coef):
    a_refs = refs[:n_a]
    w_ref, x_ref, mod_ref, o_ref = refs[n_a:]

    def update(a_ref):
        y = jnp.dot(a_ref[...], w_ref[...], preferred_element_type=F32)
        gate = mod_ref[gate_idx:gate_idx + 1, :]
        if coef != 1.0:
            gate = coef * gate
        o_ref[...] = x_ref[...] + gate * y

    if n_a == 1:
        update(a_refs[0])
    else:
        pl.when(pl.program_id(0) < split)(functools.partial(update, a_refs[0]))
        pl.when(pl.program_id(0) >= split)(functools.partial(update, a_refs[1]))


def _out_tile_n(n_a, k, d):
    for tn in (2048, 1024, 512, 256):
        blocks = 2 * (n_a * TM * k * 2 + k * tn * 2 + 2 * TM * tn * 4) + TM * tn * 4
        if d % tn == 0 and blocks <= OUT_VMEM_BUDGET:
            return tn
    raise ValueError("out_project tiles do not fit VMEM")


def _out_project(a, w, x, mod, seg, *, gate_idx, coef):
    m, d = x.shape
    assert w.dtype == BF16
    tn = _out_tile_n(len(a) if isinstance(a, tuple) else 1, w.shape[0], d)
    if isinstance(a, tuple):
        split = a[0].shape[0] // TM
        k = a[0].shape[1]
        a_specs = [pl.BlockSpec((TM, k), lambda i, j: (jnp.minimum(i, split - 1), 0)),
                   pl.BlockSpec((TM, k), lambda i, j: (jnp.maximum(i - split, 0), 0))]
    else:
        split, k = 0, a.shape[1]
        a_specs = [pl.BlockSpec((TM, k), lambda i, j: (i, 0))]
        a = (a,)
    body = functools.partial(_out_kernel, n_a=len(a), split=split, gate_idx=gate_idx, coef=coef)
    return pl.pallas_call(
        body,
        grid=(m // TM, d // tn),
        in_specs=a_specs + [pl.BlockSpec((k, tn), lambda i, j: (0, j)),
                            pl.BlockSpec((TM, tn), lambda i, j: (i, j)),
                            pl.BlockSpec((None, N_MOD, tn), lambda i, j: (seg(i), 0, j))],
        out_specs=pl.BlockSpec((TM, tn), lambda i, j: (i, j)),
        out_shape=jax.ShapeDtypeStruct((m, d), F32),
        compiler_params=_params(2),
        name="out_project",
    )(*a, w, x, mod)


def _gmlp_kernel(u_ref, v_ref, vg_ref, ws_ref, bs_ref, o_ref):
    width = v_ref.shape[1]
    gw = width // A_GROUPS
    vg = vg_ref[...]
    for c in range(v_ref.shape[0] // A_CHUNK):
        rows = slice(c * A_CHUNK, (c + 1) * A_CHUNK)
        v = v_ref[rows, :].astype(F32)
        vn = v * lax.rsqrt(jnp.mean(v * v, axis=-1, keepdims=True) + EPS)
        vn = (vn * vg).astype(BF16)
        for g in range(A_GROUPS):
            cols = slice(g * gw, (g + 1) * gw)
            sv = jnp.dot(ws_ref[g].astype(BF16), vn[:, cols], preferred_element_type=F32)
            sv = sv + bs_ref[:, g:g + 1]
            o_ref[rows, cols] = (u_ref[rows, cols].astype(F32) * sv).astype(o_ref.dtype)


def _gmlp_core(uv, v_g, w_s, b_s, j, rows_per_step=4 * A_CHUNK):
    m, two_w = uv.shape
    width = two_w // 2
    tr = rows_per_step
    return pl.pallas_call(
        _gmlp_kernel,
        grid=(m // tr,),
        in_specs=[pl.BlockSpec((tr, width), lambda i: (i, 0)),
                  pl.BlockSpec((tr, width), lambda i: (i, 1)),
                  _lead((j,), (1, width), lambda i: (0, 0)),
                  _lead((j,), (A_GROUPS, A_CHUNK, A_CHUNK), lambda i: (0, 0, 0)),
                  _lead((j,), (A_CHUNK, A_GROUPS), lambda i: (0, 0))],
        out_specs=pl.BlockSpec((tr, width), lambda i: (i, 0)),
        out_shape=jax.ShapeDtypeStruct((m, width), BF16),
        compiler_params=_params(1),
        name="gmlp_core",
    )(uv, uv, v_g[:, None, :], w_s, jnp.swapaxes(b_s, 1, 2))


def _ret_kernel(*refs, seq_len, dk, hb, has_state, has_rope, emit_state):
    it = iter(refs)
    q_ref, k_ref, v_ref, g_ref, dec_ref, gn_ref = (next(it) for _ in range(6))
    cos_ref, sin_ref = (next(it), next(it)) if has_rope else (None, None)
    s0_ref = next(it) if has_state else None
    o_ref = next(it)
    st_ref = next(it) if emit_state else None
    q_scr, k_scr, acc_scr, r_scr = (next(it) for _ in range(4))
    chunk = R_CHUNK
    n_chunks = seq_len // chunk
    dv = 2 * dk

    ri = lax.broadcasted_iota(jnp.int32, (chunk, chunk), 0).astype(F32)
    ci = lax.broadcasted_iota(jnp.int32, (chunk, chunk), 1).astype(F32)
    rel = ri - ci

    def head(hh):
        qk_cols = slice(hh * dk, (hh + 1) * dk)
        v_cols = slice(hh * dv, (hh + 1) * dv)
        q = q_ref[:, qk_cols].astype(F32)
        k = k_ref[:, qk_cols].astype(F32)
        if has_rope:
            cos, sin = cos_ref[...], sin_ref[...]

            def rope(t):
                swapped = jnp.concatenate(
                    [pltpu.roll(t[:, s:s + LANES], LANES // 2, 1) for s in range(0, dk, LANES)],
                    axis=1)
                return t * cos + swapped * sin

            q, k = rope(q), rope(k)
        q_scr[:, qk_cols] = q.astype(BF16)
        k_scr[:, qk_cols] = k * (dk ** -0.5)

        consts = []
        for direction in range(2):
            lg = _log_sigmoid(dec_ref[direction, hh])
            if direction == 0:
                dmat = jnp.where(rel >= 0, jnp.exp(jnp.maximum(rel, 0.0) * lg), 0.0)
                zeta = jnp.exp((chunk - 1 - ri) * lg)[:, :1]
                xi = jnp.exp((ri + 1) * lg)[:, :1]
            else:
                dmat = jnp.where(rel <= 0, jnp.exp(jnp.maximum(-rel, 0.0) * lg), 0.0)
                zeta = jnp.exp(ri * lg)[:, :1]
                xi = jnp.exp((chunk - ri) * lg)[:, :1]
            consts.append((dmat, zeta, xi, jnp.exp(chunk * lg)[:, :1]))
            if has_state:
                r_scr[direction, hh] = s0_ref[direction, hh].astype(F32)
            else:
                r_scr[direction, hh] = jnp.zeros((dk, dv), F32)

        def chunk_step(direction, i):
            dmat, zeta, xi, g_chunk = consts[direction]
            rows = slice(i * chunk, (i + 1) * chunk)
            qi = q_scr[rows, qk_cols]
            kf = k_scr[rows, qk_cols]
            vi = v_ref[rows, v_cols]
            s = lax.dot_general(qi, kf.astype(BF16), (((1,), (1,)), ((), ())),
                                preferred_element_type=F32)
            o_inner = jnp.dot((s * dmat).astype(BF16), vi, preferred_element_type=F32)
            o_cross = jnp.dot(qi, r_scr[direction, hh].astype(BF16),
                              preferred_element_type=F32) * xi
            kv = lax.dot_general((kf * zeta).astype(BF16), vi, (((0,), (0,)), ((), ())),
                                 preferred_element_type=F32)
            r_scr[direction, hh] = g_chunk * r_scr[direction, hh] + kv
            acc_scr[direction, rows, v_cols] = o_inner + o_cross

        for step in range(n_chunks):
            chunk_step(0, step)
            chunk_step(1, n_chunks - 1 - step)

        gn = gn_ref[:, v_cols]
        for i in range(n_chunks):
            rows = slice(i * chunk, (i + 1) * chunk)
            o = acc_scr[0, rows, v_cols] + acc_scr[1, rows, v_cols]
            mu = jnp.mean(o, axis=-1, keepdims=True)
            var = jnp.mean(jnp.square(o - mu), axis=-1, keepdims=True)
            on = ((o - mu) * lax.rsqrt(var + EPS)) * gn
            o_ref[rows, v_cols] = (jax.nn.silu(g_ref[rows, v_cols].astype(F32)) * on
                                   ).astype(o_ref.dtype)

    for hh in range(hb):
        head(hh)
    if emit_state:
        st_ref[...] = r_scr[...]


def _retention_core(proj, decay, gn_g, j, *, row0, n_seq, seq_len, state, rope, emit_state,
                    hb=2):
    heads = R_HEADS
    dk = proj.shape[1] // (6 * heads)
    dv = 2 * dk
    blk0 = row0 // seq_len
    hg = heads // hb
    dec = jnp.broadcast_to(decay[j][:, :, None, None], (2, heads, 1, R_CHUNK))
    in_specs = [pl.BlockSpec((seq_len, hb * dk), lambda b, h: (blk0 + b, h)),
                pl.BlockSpec((seq_len, hb * dk), lambda b, h: (blk0 + b, hg + h)),
                pl.BlockSpec((seq_len, hb * dv), lambda b, h: (blk0 + b, hg + h)),
                pl.BlockSpec((seq_len, hb * dv), lambda b, h: (blk0 + b, 2 * hg + h)),
                pl.BlockSpec((2, hb, 1, R_CHUNK), lambda b, h: (0, h, 0, 0)),
                _lead((j,), (1, hb * dv), lambda b, h: (0, h))]
    args = [proj, proj, proj, proj, dec, gn_g[:, None, :]]
    if rope is not None:
        in_specs += [pl.BlockSpec((seq_len, dk), lambda b, h: (0, 0))] * 2
        args += list(rope)
    if state is not None:
        in_specs.append(pl.BlockSpec((None, None, 2, hb, dk, dv),
                                     lambda b, h: (b, j, 0, h, 0, 0)))
        args.append(state)
    out_specs = [pl.BlockSpec((seq_len, hb * dv), lambda b, h: (b, h))]
    out_shape = [jax.ShapeDtypeStruct((n_seq * seq_len, heads * dv), BF16)]
    if emit_state:
        out_specs.append(pl.BlockSpec((None, 2, hb, dk, dv), lambda b, h: (b, 0, h, 0, 0)))
        out_shape.append(jax.ShapeDtypeStruct((n_seq, 2, heads, dk, dv), F32))
    body = functools.partial(_ret_kernel, seq_len=seq_len, dk=dk, hb=hb,
                             has_state=state is not None, has_rope=rope is not None,
                             emit_state=emit_state)
    return pl.pallas_call(
        body,
        grid=(n_seq, hg),
        in_specs=in_specs,
        out_specs=out_specs,
        out_shape=out_shape,
        scratch_shapes=[pltpu.VMEM((seq_len, hb * dk), BF16),
                        pltpu.VMEM((seq_len, hb * dk), F32),
                        pltpu.VMEM((2, seq_len, hb * dv), F32),
                        pltpu.VMEM((2, hb, dk, dv), F32)],
        compiler_params=_params(2),
        name="retention_core",
    )(*args)


def _rope_tables(seq_len, dk):
    nf = dk // 4
    inv = ROPE_BASE ** (-jnp.arange(nf, dtype=F32) / nf)
    t = jnp.arange(seq_len)
    tabs = []
    for p in (t // GRID_W, t % GRID_W):
        ang = p.astype(F32)[:, None] * inv
        tabs.append((jnp.cos(ang), jnp.sin(ang)))
    cos = jnp.concatenate([tabs[0][0], tabs[0][0], tabs[1][0], tabs[1][0]], axis=1)
    sin = jnp.concatenate([-tabs[0][1], tabs[0][1], -tabs[1][1], tabs[1][1]], axis=1)
    return cos, sin


def _lru_kernel(*refs, seq_len, n_blk, has_state):
    it = iter(refs)
    y_ref, x_ref, cw_ref, cb_ref, gw_ref, gb_ref, lam_ref = (next(it) for _ in range(7))
    h0_ref = next(it) if has_state else None
    o_ref, st_ref, ab_scr, run_scr, h_scr = (next(it) for _ in range(5))
    n = seq_len
    row = lax.broadcasted_iota(jnp.int32, (n, LRU_BLOCK), 0)

    def shifted(v, d, fill):
        if d == 0:
            return v
        if d > 0:
            return jnp.where(row >= d, pltpu.roll(v, d, 0), fill)
        return jnp.where(row < n + d, pltpu.roll(v, n + d, 0), fill)

    seg = n // SUBLANES
    pitch = seg + SEG_PAD
    n_slabs = x_ref.shape[1] // LANES
    slabs_per_block = LRU_BLOCK // LANES
    conv_w = cw_ref.shape[0]
    pad_l = conv_w // 2
    for blk in range(n_blk):
        bc = slice(blk * LRU_BLOCK, (blk + 1) * LRU_BLOCK)
        x = x_ref[:, bc].astype(F32)
        xc = shifted(x, pad_l, 0.0) * cw_ref[0:1, bc]
        for t in range(1, conv_w):
            xc = xc + shifted(x, pad_l - t, 0.0) * cw_ref[t:t + 1, bc]
        xc = xc + cb_ref[:, bc]
        xcb = xc.astype(BF16)
        for direction in range(2):
            gates = [jax.nn.sigmoid(
                jnp.dot(xcb, gw_ref[direction, gi, blk].astype(BF16), preferred_element_type=F32)
                + gb_ref[direction, gi:gi + 1, bc]) for gi in range(2)]
            log_a = (LRU_C * gates[0]) * _log_sigmoid(lam_ref[direction:direction + 1, bc])
            a = jnp.exp(log_a)
            b = (jnp.sqrt(-jnp.tanh(log_a) * (a * a + 1.0)) * gates[1]) * xc
            for sl in range(slabs_per_block):
                slab = blk * slabs_per_block + sl
                cols = slice(sl * LANES, (sl + 1) * LANES)
                for s in range(SUBLANES):
                    src, dst = slice(s * seg, (s + 1) * seg), slice(s * pitch, s * pitch + seg)
                    ab_scr[direction, 0, slab, dst, :] = a[src, cols]
                    ab_scr[direction, 1, slab, dst, :] = b[src, cols]

    def step_rows(i):
        return pl.ds(i, SUBLANES, stride=pitch)

    chains = [(d, sl) for d in range(2) for sl in range(n_slabs)]

    def scan_body(i, carry):
        out = []
        for (d, sl), (a_run, h_run) in zip(chains, carry):
            rows = step_rows(i if d == 0 else seg - 1 - i)
            a_i = ab_scr[d, 0, sl, rows, :]
            h_run = a_i * h_run + ab_scr[d, 1, sl, rows, :]
            a_run = a_i * a_run
            run_scr[d, 0, sl, rows, :] = a_run
            run_scr[d, 1, sl, rows, :] = h_run
            out.append((a_run, h_run))
        return tuple(out)

    init = tuple((jnp.ones((SUBLANES, LANES), F32), jnp.zeros((SUBLANES, LANES), F32))
                 for _ in chains)
    ends = lax.fori_loop(0, seg, scan_body, init, unroll=SCAN_UNROLL)

    entering = []
    for (d, sl), (a_end, h_end) in zip(chains, ends):
        cols = slice(sl * LANES, (sl + 1) * LANES)
        c = h0_ref[d:d + 1, cols] if has_state else jnp.zeros((1, LANES), F32)
        rows = [None] * SUBLANES
        for s in (range(SUBLANES) if d == 0 else range(SUBLANES - 1, -1, -1)):
            rows[s] = c
            c = h_end[s:s + 1, :] + a_end[s:s + 1, :] * c
        st_ref[d:d + 1, cols] = c
        entering.append(jnp.concatenate(rows, axis=0))

    def fix_body(i, carry):
        rows = step_rows(i)
        for sl in range(n_slabs):
            h_f, h_b = (run_scr[d, 1, sl, rows, :]
                        + run_scr[d, 0, sl, rows, :] * entering[chains.index((d, sl))]
                        for d in range(2))
            h_scr[sl, rows, :] = h_f + h_b
        return carry

    lax.fori_loop(0, seg, fix_body, 0, unroll=SCAN_UNROLL)

    for sl in range(n_slabs):
        cols = slice(sl * LANES, (sl + 1) * LANES)
        for s in range(SUBLANES):
            src, dst = slice(s * pitch, s * pitch + seg), slice(s * seg, (s + 1) * seg)
            o_ref[dst, cols] = (h_scr[sl, src, :]
                                * jax.nn.gelu(y_ref[dst, cols].astype(F32))).astype(o_ref.dtype)


def _lru_core(proj, conv_w, conv_b, gate_w, gate_b, lam, j, *, row0, n_seq, seq_len, state,
              n_blk=2):
    width = proj.shape[1] // 2
    tw = n_blk * LRU_BLOCK
    nb = width // tw
    blk0 = row0 // seq_len
    in_specs = [pl.BlockSpec((seq_len, tw), lambda b, c: (blk0 + b, c)),
                pl.BlockSpec((seq_len, tw), lambda b, c: (blk0 + b, nb + c)),
                _lead((j,), (conv_w.shape[1], tw), lambda b, c: (0, c)),
                _lead((j,), (1, tw), lambda b, c: (0, c)),
                _lead((j,), (2, 2, n_blk, LRU_BLOCK, LRU_BLOCK), lambda b, c: (0, 0, c, 0, 0)),
                _lead((j,), (2, 2, tw), lambda b, c: (0, 0, c)),
                _lead((j,), (2, tw), lambda b, c: (0, c))]
    args = [proj, proj, conv_w, conv_b[:, None, :], gate_w, gate_b, lam]
    if state is not None:
        in_specs.append(pl.BlockSpec((None, None, 2, tw), lambda b, c: (b, j, 0, c)))
        args.append(state)
    body = functools.partial(_lru_kernel, seq_len=seq_len, n_blk=n_blk,
                             has_state=state is not None)
    n_slabs = tw // LANES
    scr_rows = SUBLANES * (seq_len // SUBLANES + SEG_PAD)
    return pl.pallas_call(
        body,
        grid=(n_seq, nb),
        in_specs=in_specs,
        out_specs=[pl.BlockSpec((seq_len, tw), lambda b, c: (b, c)),
                   pl.BlockSpec((None, 2, tw), lambda b, c: (b, 0, c))],
        out_shape=[jax.ShapeDtypeStruct((n_seq * seq_len, width), BF16),
                   jax.ShapeDtypeStruct((n_seq, 2, width), F32)],
        scratch_shapes=[pltpu.VMEM((2, 2, n_slabs, scr_rows, LANES), F32),
                        pltpu.VMEM((2, 2, n_slabs, scr_rows, LANES), F32),
                        pltpu.VMEM((n_slabs, scr_rows, LANES), F32)],
        compiler_params=_params(2),
        name="lru_core",
    )(*args)


def _norm_kernel(x_ref, g_ref, o_ref):
    x = x_ref[...]
    o_ref[...] = (x * lax.rsqrt(jnp.mean(x * x, axis=-1, keepdims=True) + EPS)) * g_ref[...]


def _final_norm(x, gain, row0, rows, tr=TM):
    d = x.shape[1]
    blk0 = row0 // tr
    return pl.pallas_call(
        _norm_kernel,
        grid=(rows // tr,),
        in_specs=[pl.BlockSpec((tr, d), lambda i: (blk0 + i, 0)),
                  pl.BlockSpec((1, d), lambda i: (0, 0))],
        out_specs=pl.BlockSpec((tr, d), lambda i: (i, 0)),
        out_shape=jax.ShapeDtypeStruct((rows, d), F32),
        compiler_params=_params(1),
        name="final_norm",
    )(x, gain.reshape(1, d))


def kernel(x_prompt, x_sample, state_ret, state_lru, c, c_ctx, norm_g, final_g, w_mod, b_mod,
           ffn_w_in, ffn_w_out, a_w_in, a_v_g, a_w_s, a_b_s, a_w_out, r_w_in, r_decay, r_gn_g,
           r_w_out, l_w_in, l_conv_w, l_conv_b, l_gate_w, l_gate_b, l_lambda, l_w_out):
    batch, seq, d = x_prompt.shape
    dec_batch, dec_seq, _ = x_sample.shape
    depth = w_mod.shape[0]
    m_prompt, m_sample = batch * seq, dec_batch * dec_seq
    assert m_prompt % TM == 0 and dec_seq % TM == 0 and dec_batch + 1 <= MOD_ROWS
    prompt_tiles, tiles_per_sample = m_prompt // TM, dec_seq // TM

    def seg(i):
        return jnp.maximum(i - prompt_tiles, -1) // tiles_per_sample + 1

    x = jnp.concatenate([x_prompt.reshape(m_prompt, d), x_sample.reshape(m_sample, d)], axis=0)
    cond = jnp.concatenate(
        [c_ctx[None, :], c, jnp.zeros((MOD_ROWS - 1 - dec_batch, d), F32)], axis=0)
    mod_all = _modulation(cond, w_mod, b_mod).reshape(depth, MOD_ROWS, N_MOD, d)
    rope = _rope_tables(dec_seq, d // R_HEADS)

    def ffn(x, l, s, mod, more_casts=()):
        casts = [(ffn_w_out, (l, s), CAST_ROWS)] + list(more_casts)
        hid, w_out, *rounded = _project(x, norm_g[l, 2 * s], mod, ffn_w_in, (l, s), seg,
                                        shift_idx=6 * s, scale_idx=6 * s + 1, act="swiglu",
                                        casts=casts)
        x = _out_project(hid, w_out, x, mod, seg, gate_idx=6 * s + 2, coef=0.5)
        return (x, *rounded)

    ret_states, lru_states = [], []
    for l in range(depth):
        kind, j = l % N_MIXERS, l // N_MIXERS
        mod = mod_all[l]
        mixer_w_in, mixer_w_out = ((a_w_in, a_w_out), (r_w_in, r_w_out), (l_w_in, l_w_out))[kind]
        x, w_in = ffn(x, l, 0, mod, [(mixer_w_in, (j,), MIX_CAST_ROWS)])
        proj, w_out = _project(x, norm_g[l, 1], mod, w_in, (), seg, shift_idx=3, scale_idx=4,
                               act="gelu" if kind == 0 else "none",
                               casts=[(mixer_w_out, (j,), CAST_ROWS)])
        if kind == 0:
            mixed = _gmlp_core(proj, a_v_g, a_w_s, a_b_s, j)
        elif kind == 1:
            o_p, st = _retention_core(proj, r_decay, r_gn_g, j, row0=0, n_seq=batch,
                                      seq_len=seq, state=None, rope=None, emit_state=True,
                                      hb=4)
            (o_s,) = _retention_core(proj, r_decay, r_gn_g, j, row0=m_prompt, n_seq=dec_batch,
                                     seq_len=dec_seq, state=state_ret, rope=rope,
                                     emit_state=False)
            mixed = (o_p, o_s)
            ret_states.append(st)
        else:
            o_p, st = _lru_core(proj, l_conv_w, l_conv_b, l_gate_w, l_gate_b, l_lambda, j, row0=0,
                                n_seq=batch, seq_len=seq, state=None)
            o_s, _ = _lru_core(proj, l_conv_w, l_conv_b, l_gate_w, l_gate_b, l_lambda, j,
                               row0=m_prompt, n_seq=dec_batch, seq_len=dec_seq, state=state_lru)
            mixed = (o_p, o_s)
            lru_states.append(st)
        x = _out_project(mixed, w_out, x, mod, seg, gate_idx=5, coef=1.0)
        (x,) = ffn(x, l, 1, mod)

    y_prompt = _final_norm(x, final_g, 0, m_prompt).reshape(batch, seq, d)
    y_sample = _final_norm(x, final_g, m_prompt, m_sample).reshape(dec_batch, dec_seq, d)
    new_state_ret = jnp.stack(ret_states, axis=1)
    new_state_lru = jnp.stack(lru_states, axis=1)
    return (y_prompt, y_sample, new_state_ret, new_state_lru)
```
